```python
import math
import jax, jax.numpy as jnp
from jax import lax
import numpy as np

D_MODEL = 2048
BATCH = 4
SEQ = 8192
DEPTH = 2
DEC_BATCH = 8
DEC_SEQ = 32
PAST_LEN = 1024

CHUNK = 64
Q_BLOCK = 128
CONV_C = 512
CONV_K = 31
DIFF_H = 8
DIFF_DH = 64
DIFF_W = DIFF_H * 2 * DIFF_DH
MEM_LEN = 256
MEM_H = 4
MEM_DH = 128
MEM_W = MEM_H * MEM_DH
N_BRANCH = 3
FF = 5632
FFN_K = 3
EPS = 1e-6
N_IN = 2 * CONV_C + 3 * DIFF_W + MEM_W + N_BRANCH * D_MODEL
SPLITS = [2 * CONV_C, 2 * CONV_C + DIFF_W, 2 * CONV_C + 2 * DIFF_W,
          2 * CONV_C + 3 * DIFF_W, 2 * CONV_C + 3 * DIFF_W + MEM_W]

kernel_name = "hybrid_streaming_conv_diffattn_step"


def lambda_init(layer_idx):
    return 0.8 - 0.6 * math.exp(-0.3 * layer_idx)


def rmsnorm(x, g):
    xf = x.astype(jnp.float32)
    y = xf * lax.rsqrt(jnp.mean(xf * xf, -1, keepdims=True) + EPS)
    return (y * g.astype(jnp.float32)).astype(x.dtype)


def layernorm(x, g, b):
    xf = x.astype(jnp.float32)
    mu = jnp.mean(xf, -1, keepdims=True)
    var = jnp.mean(jnp.square(xf - mu), -1, keepdims=True)
    y = (xf - mu) * lax.rsqrt(var + EPS) * g.astype(jnp.float32) + b.astype(jnp.float32)
    return y.astype(x.dtype)


def causal_dwconv(x_padded, w, b):
    c = x_padded.shape[-1]
    y = lax.conv_general_dilated(x_padded, w[:, None, :].astype(x_padded.dtype),
                                 window_strides=(1,), padding='VALID',
                                 dimension_numbers=('NWC', 'WIO', 'NWC'),
                                 feature_group_count=c)
    return y + b.astype(y.dtype)


def diff_attend(q, k, v, lam, mask):
    scale = DIFF_DH ** -0.5
    q1, q2 = jnp.split(q, 2, -1)
    k1, k2 = jnp.split(k, 2, -1)

    def probs(qa, ka):
        s = jnp.einsum('bqhd,bkhd->bhqk', qa, ka).astype(jnp.float32) * scale
        if mask is not None:
            s = jnp.where(mask, s, -jnp.inf)
        return jax.nn.softmax(s, axis=-1)

    a = probs(q1, k1) - lam * probs(q2, k2)
    return jnp.einsum('bhqk,bkhd->bqhd', a.astype(v.dtype), v)


def diff_attn_prompt(q, k, v, lam):
    b, s = q.shape[:2]
    nb = s // Q_BLOCK
    qb = q.reshape(b, nb, Q_BLOCK, DIFF_H, 2 * DIFF_DH).transpose(1, 0, 2, 3, 4)
    k_chunk = jnp.arange(s) // CHUNK

    def block(args):
        i, qi = args
        q_chunk = (i * Q_BLOCK + jnp.arange(Q_BLOCK)) // CHUNK
        mask = k_chunk[None, :] <= q_chunk[:, None]
        return diff_attend(qi, k, v, lam, mask)

    o = lax.map(block, (jnp.arange(nb), qb))
    return o.transpose(1, 0, 2, 3, 4).reshape(b, s, DIFF_H, 2 * DIFF_DH)


def mem_kv(mem, g, w):
    b, m = mem.shape[:2]
    kv = rmsnorm(mem, g) @ w
    mk, mv = jnp.split(kv, 2, -1)
    return mk.reshape(b, m, MEM_H, MEM_DH), mv.reshape(b, m, MEM_H, MEM_DH)


def mem_attend(q, mk, mv):
    s = jnp.einsum('bthd,bmhd->bhtm', q, mk).astype(jnp.float32) * (MEM_DH ** -0.5)
    p = jax.nn.softmax(s, axis=-1).astype(mv.dtype)
    return jnp.einsum('bhtm,bmhd->bthd', p, mv)


def layer(x, layer_idx, conv_ctx, ffn_ctx, k_past, v_past, mk, mv, P):
    bsz, t = x.shape[:2]
    h = rmsnorm(x, P['norm1_g'])
    z = h @ P['w_in']
    conv_in, q, k, v, mq, gates = jnp.split(z, SPLITS, axis=-1)

    ga, gb = jnp.split(conv_in, 2, -1)
    u = ga * jax.nn.sigmoid(gb)
    u_full = jnp.concatenate([conv_ctx.astype(u.dtype), u], axis=1)
    new_conv = u_full[:, -(CONV_K - 1):]
    c = causal_dwconv(u_full, P['conv_dw_w'], P['conv_dw_b'])
    c = jax.nn.silu(layernorm(c, P['conv_ln_g'], P['conv_ln_b']))
    y_a = c @ P['w_conv_out']

    q = q.reshape(bsz, t, DIFF_H, 2 * DIFF_DH)
    k = k.reshape(bsz, t, DIFF_H, 2 * DIFF_DH)
    v = v.reshape(bsz, t, DIFF_H, 2 * DIFF_DH)
    lq1, lk1 = P['lam_q1'].astype(jnp.float32), P['lam_k1'].astype(jnp.float32)
    lq2, lk2 = P['lam_q2'].astype(jnp.float32), P['lam_k2'].astype(jnp.float32)
    l_init = lambda_init(layer_idx)
    lam = jnp.exp(jnp.sum(lq1 * lk1)) - jnp.exp(jnp.sum(lq2 * lk2)) + l_init
    if k_past is None:
        o = diff_attn_prompt(q, k, v, lam)
    else:
        k_all = jnp.concatenate([k_past.astype(k.dtype), k], axis=1)
        v_all = jnp.concatenate([v_past.astype(v.dtype), v], axis=1)
        o = diff_attend(q, k_all, v_all, lam, None)
    o = rmsnorm(o, P['diff_subln_g']) * (1.0 - l_init)
    y_b = o.reshape(bsz, t, DIFF_W) @ P['w_diff_out']

    om = mem_attend(mq.reshape(bsz, t, MEM_H, MEM_DH), mk.astype(mq.dtype), mv.astype(mq.dtype))
    y_m = om.reshape(bsz, t, MEM_W) @ P['w_mem_out']

    g_a, g_b, g_m = jnp.split(jax.nn.sigmoid(gates), N_BRANCH, -1)
    x = x + (g_a * y_a + g_b * y_b + g_m * y_m) @ P['w_out']

    h2 = rmsnorm(x, P['norm2_g'])
    fg, fu = jnp.split(h2 @ P['w_ffn_gu'], 2, -1)
    fg_full = jnp.concatenate([ffn_ctx.astype(fg.dtype), fg], axis=1)
    new_ffn = fg_full[:, -(FFN_K - 1):]
    fg = causal_dwconv(fg_full, P['ffn_dw_w'], P['ffn_dw_b'])
    x = x + (jax.nn.silu(fg) * fu) @ P['w_ffn_down']
    return x, k, v, new_conv, new_ffn


def setup_inputs(seed: int = 0) -> dict:
    key = jax.random.key(seed)
    ks = jax.random.split(key, 40)
    f32 = jnp.float32

    def nrm(i, shape, scale):
        return jax.random.normal(ks[i], shape, f32) * scale

    def gain(i, shape):
        return 1.0 + 0.01 * jax.random.normal(ks[i], shape, f32)

    return {
        "x_prompt": nrm(0, (BATCH, SEQ, D_MODEL), 1.0),
        "x_sample": nrm(1, (DEC_BATCH, DEC_SEQ, D_MODEL), 1.0),
        "mem_prompt": nrm(2, (BATCH, MEM_LEN, D_MODEL), 1.0),
        "cache_k": nrm(3, (DEPTH, DEC_BATCH, PAST_LEN, DIFF_H, 2 * DIFF_DH), 1.0),
        "cache_v": nrm(4, (DEPTH, DEC_BATCH, PAST_LEN, DIFF_H, 2 * DIFF_DH), 1.0),
        "cache_mem_k": nrm(5, (DEPTH, DEC_BATCH, MEM_LEN, MEM_H, MEM_DH), 1.0),
        "cache_mem_v": nrm(6, (DEPTH, DEC_BATCH, MEM_LEN, MEM_H, MEM_DH), 1.0),
        "state_conv": nrm(7, (DEPTH, DEC_BATCH, CONV_K - 1, CONV_C), 0.5),
        "state_ffn_conv": nrm(8, (DEPTH, DEC_BATCH, FFN_K - 1, FF), 1.0),
        "norm1_g": gain(9, (DEPTH, D_MODEL)),
        "w_in": nrm(10, (DEPTH, D_MODEL, N_IN), D_MODEL ** -0.5),
        "conv_dw_w": nrm(11, (DEPTH, CONV_K, CONV_C), CONV_K ** -0.5),
        "conv_dw_b": nrm(12, (DEPTH, CONV_C), 0.01),
        "conv_ln_g": gain(13, (DEPTH, CONV_C)),
        "conv_ln_b": nrm(14, (DEPTH, CONV_C), 0.01),
        "w_conv_out": nrm(15, (DEPTH, CONV_C, D_MODEL), CONV_C ** -0.5),
        "lam_q1": nrm(16, (DEPTH, DIFF_DH), 0.1),
        "lam_k1": nrm(17, (DEPTH, DIFF_DH), 0.1),
        "lam_q2": nrm(18, (DEPTH, DIFF_DH), 0.1),
        "lam_k2": nrm(19, (DEPTH, DIFF_DH), 0.1),
        "diff_subln_g": gain(20, (DEPTH, 2 * DIFF_DH)),
        "w_diff_out": nrm(21, (DEPTH, DIFF_W, D_MODEL), DIFF_W ** -0.5),
        "mem_norm_g": gain(22, (DEPTH, D_MODEL)),
        "w_mem_kv": nrm(23, (DEPTH, D_MODEL, 2 * MEM_W), D_MODEL ** -0.5),
        "w_mem_out": nrm(24, (DEPTH, MEM_W, D_MODEL), MEM_W ** -0.5),
        "w_out": nrm(25, (DEPTH, D_MODEL, D_MODEL), D_MODEL ** -0.5),
        "norm2_g": gain(26, (DEPTH, D_MODEL)),
        "w_ffn_gu": nrm(27, (DEPTH, D_MODEL, 2 * FF), D_MODEL ** -0.5),
        "ffn_dw_w": nrm(28, (DEPTH, FFN_K, FF), FFN_K ** -0.5),
        "ffn_dw_b": nrm(29, (DEPTH, FF), 0.01),
        "w_ffn_down": nrm(30, (DEPTH, FF, D_MODEL), FF ** -0.5),
        "final_g": gain(31, (D_MODEL,)),
    }


def reference(x_prompt, x_sample, mem_prompt, cache_k, cache_v, cache_mem_k, cache_mem_v,
              state_conv, state_ffn_conv, norm1_g, w_in, conv_dw_w, conv_dw_b, conv_ln_g,
              conv_ln_b, w_conv_out, lam_q1, lam_k1, lam_q2, lam_k2, diff_subln_g, w_diff_out,
              mem_norm_g, w_mem_kv, w_mem_out, w_out, norm2_g, w_ffn_gu, ffn_dw_w, ffn_dw_b,
              w_ffn_down, final_g):
    xp, xs = x_prompt, x_sample
    kp_l, vp_l, mkp_l, mvp_l, cp_l, fp_l = [], [], [], [], [], []
    ks_l, vs_l, cs_l, fs_l = [], [], [], []
    for l in range(DEPTH):
        P = {
            'norm1_g': norm1_g[l], 'w_in': w_in[l], 'conv_dw_w': conv_dw_w[l],
            'conv_dw_b': conv_dw_b[l], 'conv_ln_g': conv_ln_g[l], 'conv_ln_b': conv_ln_b[l],
            'w_conv_out': w_conv_out[l], 'lam_q1': lam_q1[l], 'lam_k1': lam_k1[l],
            'lam_q2': lam_q2[l], 'lam_k2': lam_k2[l], 'diff_subln_g': diff_subln_g[l],
            'w_diff_out': w_diff_out[l], 'w_mem_out': w_mem_out[l], 'w_out': w_out[l],
            'norm2_g': norm2_g[l], 'w_ffn_gu': w_ffn_gu[l], 'ffn_dw_w': ffn_dw_w[l],
            'ffn_dw_b': ffn_dw_b[l], 'w_ffn_down': w_ffn_down[l],
        }
        mk_p, mv_p = mem_kv(mem_prompt, mem_norm_g[l], w_mem_kv[l])
        conv0 = jnp.zeros((xp.shape[0], CONV_K - 1, CONV_C), xp.dtype)
        ffn0 = jnp.zeros((xp.shape[0], FFN_K - 1, FF), xp.dtype)
        xp, kp, vp, cp, fp = layer(xp, l, conv0, ffn0, None, None, mk_p, mv_p, P)
        kp_l.append(kp); vp_l.append(vp); mkp_l.append(mk_p); mvp_l.append(mv_p)
        cp_l.append(cp); fp_l.append(fp)
        xs, ks_, vs_, cs_, fs_ = layer(xs, l, state_conv[l], state_ffn_conv[l], cache_k[l],
                                       cache_v[l], cache_mem_k[l], cache_mem_v[l], P)
        ks_l.append(ks_); vs_l.append(vs_); cs_l.append(cs_); fs_l.append(fs_)
    y_prompt = rmsnorm(xp, final_g)
    y_sample = rmsnorm(xs, final_g)
    return (y_prompt, y_sample,
            jnp.stack(kp_l), jnp.stack(vp_l), jnp.stack(mkp_l), jnp.stack(mvp_l),
            jnp.stack(cp_l), jnp.stack(fp_l),
            jnp.stack(ks_l), jnp.stack(vs_l), jnp.stack(cs_l), jnp.stack(fs_l))
```

```python
import functools
import math

import jax
import jax.numpy as jnp
from jax import lax
from jax.experimental import pallas as pl
from jax.experimental.pallas import tpu as pltpu

F32 = jnp.float32
BF16 = jnp.bfloat16

EPS = 1e-6
CHUNK = 64
DIFF_H = 8
DIFF_DH = 64
HEAD_W = 2 * DIFF_DH
DIFF_W = DIFF_H * HEAD_W
MEM_H = 4
MEM_DH = 128
MEM_W = MEM_H * MEM_DH
N_BRANCH = 3
STATE_PAD = 8

VMEM_LIMIT_BYTES = 56 * 1024 * 1024


def _lambda_init(layer_idx):
    return 0.8 - 0.6 * math.exp(-0.3 * layer_idx)


def _params(*sem):
    return pltpu.CompilerParams(dimension_semantics=sem, vmem_limit_bytes=VMEM_LIMIT_BYTES)


def _pick(n, prefs):
    for p in prefs:
        if n % p == 0:
            return p
    return n


def _rms_rows(x, g):
    return x * lax.rsqrt(jnp.mean(x * x, axis=-1, keepdims=True) + EPS) * g


def _kv_proj_kernel(x_ref, g_ref, w_ref, k_ref, v_ref, kb_ref, vb_ref):
    h = _rms_rows(x_ref[...], g_ref[...]).astype(BF16)
    kv = jnp.dot(h, w_ref[...], preferred_element_type=F32)
    k = kv[:, :DIFF_W]
    v = kv[:, DIFF_W:]
    k_ref[...] = k
    v_ref[...] = v
    kb_ref[...] = k.astype(BF16)
    vb_ref[...] = v.astype(BF16)


def _kv_proj(x, g, w_kv):
    t, d = x.shape
    tm = _pick(t, (512, 256))
    row = lambda i: (i, 0)
    const = lambda i: (0, 0)
    out_blk = pl.BlockSpec((tm, DIFF_W), row)
    return pl.pallas_call(
        _kv_proj_kernel,
        grid=(t // tm,),
        in_specs=[pl.BlockSpec((tm, d), row), pl.BlockSpec((1, d), const),
                  pl.BlockSpec((d, 2 * DIFF_W), const)],
        out_specs=[out_blk, out_blk, out_blk, out_blk],
        out_shape=[jax.ShapeDtypeStruct((t, DIFF_W), F32), jax.ShapeDtypeStruct((t, DIFF_W), F32),
                   jax.ShapeDtypeStruct((t, DIFF_W), BF16), jax.ShapeDtypeStruct((t, DIFF_W), BF16)],
        compiler_params=_params("parallel"),
        name="kv_proj",
    )(x, g, w_kv)


def _norm_proj_kernel(x_ref, g_ref, w_ref, o_ref, h_ref):
    @pl.when(pl.program_id(1) == 0)
    def _():
        h_ref[...] = _rms_rows(x_ref[...], g_ref[...]).astype(BF16)

    o_ref[...] = jnp.dot(h_ref[...], w_ref[...], preferred_element_type=F32).astype(o_ref.dtype)


def _norm_proj(x, g, w, out_dtype):
    t, d = x.shape
    n = w.shape[1]
    tm = _pick(t, (1024, 512, 256))
    tn = _pick(n, (512, 256, 128))
    return pl.pallas_call(
        _norm_proj_kernel,
        grid=(t // tm, n // tn),
        in_specs=[pl.BlockSpec((tm, d), lambda i, j: (i, 0)),
                  pl.BlockSpec((1, d), lambda i, j: (0, 0)),
                  pl.BlockSpec((d, tn), lambda i, j: (0, j))],
        out_specs=pl.BlockSpec((tm, tn), lambda i, j: (i, j)),
        out_shape=jax.ShapeDtypeStruct((t, n), out_dtype),
        scratch_shapes=[pltpu.VMEM((tm, d), BF16)],
        compiler_params=_params("parallel", "arbitrary"),
        name="norm_proj",
    )(x, g, w)


CONV_ROWS = 32


def _conv_branch_kernel(cin_ref, ctx_ref, w_ref, b_ref, lg_ref, lb_ref, c_ref, st_ref, ubuf,
                        *, tm, cc, ktaps, hist):
    s = pl.program_id(1)
    lead = hist - (ktaps - 1)

    @pl.when(s == 0)
    def _():
        ubuf[0:hist, :] = ctx_ref[...]

    cin = cin_ref[...].astype(F32)
    u = cin[:, :cc] * jax.nn.sigmoid(cin[:, cc:])
    ubuf[hist:hist + tm, :] = u

    bias = b_ref[...]
    lg = lg_ref[...]
    lb = lb_ref[...]
    for r0 in range(0, tm, CONV_ROWS):
        acc = jnp.zeros((CONV_ROWS, cc), F32) + bias
        for k in range(ktaps):
            a = lead + k + r0
            acc = acc + w_ref[k:k + 1, :] * ubuf[a:a + CONV_ROWS, :]
        mu = jnp.mean(acc, axis=-1, keepdims=True)
        xc = acc - mu
        var = jnp.mean(xc * xc, axis=-1, keepdims=True)
        y = xc * lax.rsqrt(var + EPS) * lg + lb
        c_ref[r0:r0 + CONV_ROWS, :] = (y * jax.nn.sigmoid(y)).astype(c_ref.dtype)

    tail = ubuf[tm:tm + hist, :]
    st_ref[...] = tail
    ubuf[0:hist, :] = tail


def _conv_branch(slab, col_blk, ctx, dw_w, dw_b, ln_g, ln_b, bsz, seq):
    ktaps, cc = dw_w.shape
    hist = 32
    assert ktaps - 1 <= hist and ctx.shape == (bsz, hist, cc)
    tm = _pick(seq, (512, 256, 128, 64, 32))
    nt = seq // tm
    kern = functools.partial(_conv_branch_kernel, tm=tm, cc=cc, ktaps=ktaps, hist=hist)
    vec = pl.BlockSpec((1, cc), lambda b, s: (0, 0))
    return pl.pallas_call(
        kern,
        grid=(bsz, nt),
        in_specs=[pl.BlockSpec((tm, 2 * cc), lambda b, s: (b * nt + s, col_blk)),
                  pl.BlockSpec((None, hist, cc), lambda b, s: (b, 0, 0)),
                  pl.BlockSpec((ktaps, cc), lambda b, s: (0, 0)),
                  vec, vec, vec],
        out_specs=[pl.BlockSpec((tm, cc), lambda b, s: (b * nt + s, 0)),
                   pl.BlockSpec((None, hist, cc), lambda b, s: (b, 0, 0))],
        out_shape=[jax.ShapeDtypeStruct((bsz * seq, cc), BF16),
                   jax.ShapeDtypeStruct((bsz, hist, cc), F32)],
        scratch_shapes=[pltpu.VMEM((hist + tm, cc), F32)],
        compiler_params=_params("arbitrary", "arbitrary"),
        name="conv_branch",
    )(slab, ctx, dw_w, dw_b, ln_g, ln_b)


def _diff_attn_kernel(q_ref, k_ref, v_ref, lq1_ref, lk1_ref, lq2_ref, lk2_ref, g_ref, o_ref,
                      qq_ref, m_ref, l_ref, acc_ref, *, bq, bk, nk, causal, l_init):
    i = pl.program_id(2)
    lane = lax.broadcasted_iota(jnp.int32, (bq, HEAD_W), 1)
    q = q_ref[...] * (DIFF_DH ** -0.5)
    zero = jnp.zeros_like(q)
    qq_ref[0:bq, :] = jnp.where(lane < DIFF_DH, q, zero)
    qq_ref[bq:2 * bq, :] = jnp.where(lane >= DIFF_DH, q, zero)
    m_ref[...] = jnp.full(m_ref.shape, -jnp.inf, F32)
    l_ref[...] = jnp.zeros(l_ref.shape, F32)
    acc_ref[...] = jnp.zeros(acc_ref.shape, F32)

    def block(j, masked):
        start = j * bk if isinstance(j, int) else pl.multiple_of(j * bk, bk)
        kblk = k_ref[pl.ds(start, bk), :]
        vblk = v_ref[pl.ds(start, bk), :]
        s = lax.dot_general(qq_ref[...], kblk, (((1,), (1,)), ((), ())),
                            preferred_element_type=F32)
        if masked:
            row = lax.broadcasted_iota(jnp.int32, (2 * bq, bk), 0)
            col = lax.broadcasted_iota(jnp.int32, (2 * bq, bk), 1)
            qrow = jnp.where(row >= bq, row - bq, row)
            s = jnp.where(col // CHUNK <= qrow // CHUNK, s, -jnp.inf)
        m_prev = m_ref[...]
        m_new = jnp.maximum(m_prev, jnp.max(s, axis=-1, keepdims=True))
        alpha = jnp.exp(m_prev - m_new)
        p = jnp.exp(s - m_new)
        l_ref[...] = alpha * l_ref[...] + jnp.sum(p, axis=-1, keepdims=True)
        acc_ref[...] = alpha * acc_ref[...] + jnp.dot(p.astype(BF16), vblk,
                                                      preferred_element_type=F32)
        m_ref[...] = m_new

    if causal:
        def body(j, carry):
            block(j, False)
            return carry
        lax.fori_loop(0, i, body, 0)
        block(i, True)
    else:
        for j in range(nk):
            block(j, False)

    lam = (jnp.exp(jnp.sum(lq1_ref[...] * lk1_ref[...], axis=-1, keepdims=True))
           - jnp.exp(jnp.sum(lq2_ref[...] * lk2_ref[...], axis=-1, keepdims=True)) + l_init)
    o_all = acc_ref[...] / l_ref[...]
    o = o_all[0:bq, :] - lam * o_all[bq:2 * bq, :]
    o_ref[...] = (_rms_rows(o, g_ref[...]) * (1.0 - l_init)).astype(o_ref.dtype)


def _diff_attn(slab, q_col0, kb, vb, lam_p, subln_g, bsz, sq, causal, l_init):
    sk = kb.shape[1]
    if causal:
        assert sq == sk
        bq = bk = _pick(sq, (256, 128, 64))
        assert bq % CHUNK == 0
    else:
        bq, bk = sq, sk
    nq, nk = sq // bq, sk // bk
    kern = functools.partial(_diff_attn_kernel, bq=bq, bk=bk, nk=nk, causal=causal, l_init=l_init)
    lam_blk = pl.BlockSpec((1, DIFF_DH), lambda b, h, i: (0, 0))
    kv_blk = pl.BlockSpec((None, sk, HEAD_W), lambda b, h, i: (b, 0, h))
    return pl.pallas_call(
        kern,
        grid=(bsz, DIFF_H, nq),
        in_specs=[pl.BlockSpec((bq, HEAD_W), lambda b, h, i: (b * nq + i, q_col0 + h)),
                  kv_blk, kv_blk, lam_blk, lam_blk, lam_blk, lam_blk,
                  pl.BlockSpec((1, HEAD_W), lambda b, h, i: (0, 0))],
        out_specs=pl.BlockSpec((bq, HEAD_W), lambda b, h, i: (b * nq + i, h)),
        out_shape=jax.ShapeDtypeStruct((bsz * sq, DIFF_W), BF16),
        scratch_shapes=[pltpu.VMEM((2 * bq, HEAD_W), BF16),
                        pltpu.VMEM((2 * bq, 1), F32),
                        pltpu.VMEM((2 * bq, 1), F32),
                        pltpu.VMEM((2 * bq, HEAD_W), F32)],
        compiler_params=_params("parallel", "parallel", "arbitrary"),
        name="diff_attn",
    )(slab, kb, vb, *lam_p, subln_g)


def _merge_kernel(x_ref, c_ref, on_ref, mq_ref, gate_ref, mk_ref, mv_ref,
                  wco_ref, wdo_ref, wmo_ref, wout_ref, o_ref, om_ref, *, d):
    scale = MEM_DH ** -0.5
    for h in range(MEM_H):
        cols = slice(h * MEM_DH, (h + 1) * MEM_DH)
        s = lax.dot_general(mq_ref[:, cols], mk_ref[:, cols], (((1,), (1,)), ((), ())),
                            preferred_element_type=F32) * scale
        p = jnp.exp(s - jnp.max(s, axis=-1, keepdims=True))
        p = p / jnp.sum(p, axis=-1, keepdims=True)
        om_ref[:, cols] = jnp.dot(p.astype(BF16), mv_ref[:, cols],
                                  preferred_element_type=F32).astype(BF16)

    def gate(b):
        return jax.nn.sigmoid(gate_ref[:, b * d:(b + 1) * d].astype(F32))

    mix = gate(0) * jnp.dot(c_ref[...], wco_ref[...], preferred_element_type=F32)
    mix = mix + gate(1) * jnp.dot(on_ref[...], wdo_ref[...], preferred_element_type=F32)
    mix = mix + gate(2) * jnp.dot(om_ref[...], wmo_ref[...], preferred_element_type=F32)
    o_ref[...] = x_ref[...] + jnp.dot(mix.astype(BF16), wout_ref[...], preferred_element_type=F32)


def _merge(x, c, on, slab, mq_blk, mk, mv, wco, wdo, wmo, wout, bsz, seq):
    t, d = x.shape
    cc = c.shape[1]
    mlen = mk.shape[1]
    tm = _pick(seq, (256, 128, 64, 32))
    nt = seq // tm
    row = lambda i: (i, 0)
    full = lambda a: pl.BlockSpec(a.shape, lambda i: (0, 0))
    mem_blk = pl.BlockSpec((None, mlen, MEM_W), lambda i: (i // nt, 0, 0))
    return pl.pallas_call(
        functools.partial(_merge_kernel, d=d),
        grid=(t // tm,),
        in_specs=[pl.BlockSpec((tm, d), row), pl.BlockSpec((tm, cc), row),
                  pl.BlockSpec((tm, DIFF_W), row),
                  pl.BlockSpec((tm, MEM_W), lambda i: (i, mq_blk)),
                  pl.BlockSpec((tm, N_BRANCH * d), row),
                  mem_blk, mem_blk, full(wco), full(wdo), full(wmo), full(wout)],
        out_specs=pl.BlockSpec((tm, d), row),
        out_shape=jax.ShapeDtypeStruct((t, d), F32),
        scratch_shapes=[pltpu.VMEM((tm, MEM_W), BF16)],
        compiler_params=_params("parallel"),
        name="merge",
    )(x, c, on, slab, slab, mk, mv, wco, wdo, wmo, wout)


def _ffn_body(x_ref, g_ref, wg_ref, wu_ref, wd_ref, dw_ref, db_ref, ctx_ref, fin_ref,
              o_ref, st_ref, h_ref, acc_ref, ext_ref, tail_ref, *, tm, nt, nf, final_norm):
    i = pl.program_id(0)
    j = pl.program_id(1)
    s = i % nt

    @pl.when(j == 0)
    def _():
        h_ref[...] = _rms_rows(x_ref[...], g_ref[...]).astype(BF16)
        acc_ref[...] = jnp.zeros(acc_ref.shape, F32)

    h = h_ref[...]
    fg = jnp.dot(h, wg_ref[...], preferred_element_type=F32)
    fu = jnp.dot(h, wu_ref[...], preferred_element_type=F32)

    @pl.when(s == 0)
    def _():
        ext_ref[0:STATE_PAD, :] = ctx_ref[...]

    @pl.when(s != 0)
    def _():
        ext_ref[0:STATE_PAD, :] = tail_ref[j]

    ext_ref[STATE_PAD:STATE_PAD + tm, :] = fg
    last = fg[tm - STATE_PAD:tm, :]
    tail_ref[j] = last
    st_ref[...] = last

    y = (dw_ref[0:1, :] * ext_ref[STATE_PAD - 2:STATE_PAD - 2 + tm, :]
         + dw_ref[1:2, :] * ext_ref[STATE_PAD - 1:STATE_PAD - 1 + tm, :]
         + dw_ref[2:3, :] * fg + db_ref[...])
    act = (y * jax.nn.sigmoid(y) * fu).astype(BF16)
    acc_ref[...] += jnp.dot(act, wd_ref[...], preferred_element_type=F32)

    @pl.when(j == nf - 1)
    def _():
        out = x_ref[...] + acc_ref[...]
        if final_norm:
            out = _rms_rows(out, fin_ref[...])
        o_ref[...] = out


def _ffn(x, g, wgu, wd, dw_w, dw_b, ctx, fin_g, bsz, seq, final_norm):
    t, d = x.shape
    ff = wd.shape[0]
    assert dw_w.shape[0] == 3 and ctx.shape == (bsz, STATE_PAD, ff)
    tm = _pick(seq, (512, 256, 128, 64, 32))
    tf = _pick(ff, (512, 256, 128))
    nt, nf = seq // tm, ff // tf
    kern = functools.partial(_ffn_body, tm=tm, nt=nt, nf=nf, final_norm=final_norm)
    return pl.pallas_call(
        kern,
        grid=(t // tm, nf),
        in_specs=[pl.BlockSpec((tm, d), lambda i, j: (i, 0)),
                  pl.BlockSpec((1, d), lambda i, j: (0, 0)),
                  pl.BlockSpec((d, tf), lambda i, j: (0, j)),
                  pl.BlockSpec((d, tf), lambda i, j: (0, nf + j)),
                  pl.BlockSpec((tf, d), lambda i, j: (j, 0)),
                  pl.BlockSpec((3, tf), lambda i, j: (0, j)),
                  pl.BlockSpec((1, tf), lambda i, j: (0, j)),
                  pl.BlockSpec((None, STATE_PAD, tf), lambda i, j: (i // nt, 0, j)),
                  pl.BlockSpec((1, d), lambda i, j: (0, 0))],
        out_specs=[pl.BlockSpec((tm, d), lambda i, j: (i, 0)),
                   pl.BlockSpec((None, STATE_PAD, tf), lambda i, j: (i // nt, 0, j))],
        out_shape=[jax.ShapeDtypeStruct((t, d), F32),
                   jax.ShapeDtypeStruct((bsz, STATE_PAD, ff), F32)],
        scratch_shapes=[pltpu.VMEM((tm, d), BF16), pltpu.VMEM((tm, d), F32),
                        pltpu.VMEM((STATE_PAD + tm, tf), F32),
                        pltpu.VMEM((nf, STATE_PAD, tf), F32)],
        compiler_params=_params("arbitrary", "arbitrary"),
        name="conv_ffn",
    )(x, g, wgu, wgu, wd, dw_w, dw_b, ctx, fin_g)


def _row(v):
    return v.reshape(1, -1)


def _pad_state(st, rows):
    return jnp.pad(st, ((0, 0), (rows - st.shape[1], 0), (0, 0)))


def _layer(x, bsz, seq, layer_idx, conv_ctx, ffn_ctx, k_past, v_past, mk, mv, P, final_g, final_norm):
    t, d = x.shape
    cc = P["conv_dw_w"].shape[1]
    l_init = _lambda_init(layer_idx)

    k, v, kb, vb = _kv_proj(x, P["norm1_g"], P["w_kv"])
    slab = _norm_proj(x, P["norm1_g"], P["w_main"], BF16)
    gate_w = N_BRANCH * d
    conv_blk = gate_w // (2 * cc)
    q_col0 = (gate_w + 2 * cc) // HEAD_W
    mq_blk = (gate_w + 2 * cc + DIFF_W) // MEM_W
    assert gate_w % (2 * cc) == 0 and (gate_w + 2 * cc + DIFF_W) % MEM_W == 0

    c, conv_state = _conv_branch(slab, conv_blk, conv_ctx, P["conv_dw_w"], P["conv_dw_b"],
                                 P["conv_ln_g"], P["conv_ln_b"], bsz, seq)

    kb3 = kb.reshape(bsz, seq, DIFF_W)
    vb3 = vb.reshape(bsz, seq, DIFF_W)
    if k_past is not None:
        kb3 = jnp.concatenate([k_past, kb3], axis=1)
        vb3 = jnp.concatenate([v_past, vb3], axis=1)
    on = _diff_attn(slab, q_col0, kb3, vb3, P["lam"], P["diff_subln_g"], bsz, seq,
                    causal=k_past is None, l_init=l_init)

    x1 = _merge(x, c, on, slab, mq_blk, mk, mv, P["w_conv_out"], P["w_diff_out"],
                P["w_mem_out"], P["w_out"], bsz, seq)
    x2, ffn_state = _ffn(x1, P["norm2_g"], P["w_ffn_gu"], P["w_ffn_down"], P["ffn_dw_w"],
                         P["ffn_dw_b"], ffn_ctx, final_g, bsz, seq, final_norm)
    return x2, k, v, conv_state, ffn_state


def kernel(x_prompt, x_sample, mem_prompt, cache_k, cache_v, cache_mem_k, cache_mem_v, state_conv, state_ffn_conv, norm1_g, w_in, conv_dw_w, conv_dw_b, conv_ln_g, conv_ln_b, w_conv_out, lam_q1, lam_k1, lam_q2, lam_k2, diff_subln_g, w_diff_out, mem_norm_g, w_mem_kv, w_mem_out, w_out, norm2_g, w_ffn_gu, ffn_dw_w, ffn_dw_b, w_ffn_down, final_g):
    depth = w_in.shape[0]
    bp, sp, d = x_prompt.shape
    bs, ss, _ = x_sample.shape
    mlen = mem_prompt.shape[1]
    ktaps, cc = conv_dw_w.shape[1:]
    ff = w_ffn_down.shape[1]
    past = cache_k.shape[2]
    conv_hist = 32

    xp = x_prompt.reshape(bp * sp, d)
    xs = x_sample.reshape(bs * ss, d)
    mem = mem_prompt.reshape(bp * mlen, d)
    fin = _row(final_g)

    c0, c1, c2, c3, c4 = (2 * cc, 2 * cc + DIFF_W, 2 * cc + 2 * DIFF_W, 2 * cc + 3 * DIFF_W,
                          2 * cc + 3 * DIFF_W + MEM_W)
    outs = {n: [] for n in ("kp", "vp", "mkp", "mvp", "cp", "fp", "ks", "vs", "cs", "fs")}
    for l in range(depth):
        wl = w_in[l].astype(BF16)
        P = {
            "norm1_g": _row(norm1_g[l]),
            "w_kv": wl[:, c1:c3],
            "w_main": jnp.concatenate([wl[:, c4:], wl[:, :c0], wl[:, c0:c1], wl[:, c3:c4]], axis=1),
            "conv_dw_w": conv_dw_w[l], "conv_dw_b": _row(conv_dw_b[l]),
            "conv_ln_g": _row(conv_ln_g[l]), "conv_ln_b": _row(conv_ln_b[l]),
            "w_conv_out": w_conv_out[l].astype(BF16),
            "lam": (_row(lam_q1[l]), _row(lam_k1[l]), _row(lam_q2[l]), _row(lam_k2[l])),
            "diff_subln_g": _row(diff_subln_g[l]),
            "w_diff_out": w_diff_out[l].astype(BF16),
            "w_mem_out": w_mem_out[l].astype(BF16),
            "w_out": w_out[l].astype(BF16),
            "norm2_g": _row(norm2_g[l]),
            "w_ffn_gu": w_ffn_gu[l].astype(BF16),
            "ffn_dw_w": ffn_dw_w[l], "ffn_dw_b": _row(ffn_dw_b[l]),
            "w_ffn_down": w_ffn_down[l].astype(BF16),
        }
        final_norm = l == depth - 1

        mkv = _norm_proj(mem, _row(mem_norm_g[l]), w_mem_kv[l].astype(BF16), F32)
        mk_p = mkv[:, :MEM_W].reshape(bp, mlen, MEM_W)
        mv_p = mkv[:, MEM_W:].reshape(bp, mlen, MEM_W)
        xp, kp, vp, cp, fp = _layer(
            xp, bp, sp, l, jnp.zeros((bp, conv_hist, cc), F32), jnp.zeros((bp, STATE_PAD, ff), F32),
            None, None, mk_p.astype(BF16), mv_p.astype(BF16), P, fin, final_norm)
        outs["kp"].append(kp.reshape(bp, sp, DIFF_H, HEAD_W))
        outs["vp"].append(vp.reshape(bp, sp, DIFF_H, HEAD_W))
        outs["mkp"].append(mk_p.reshape(bp, mlen, MEM_H, MEM_DH))
        outs["mvp"].append(mv_p.reshape(bp, mlen, MEM_H, MEM_DH))
        outs["cp"].append(cp[:, conv_hist - (ktaps - 1):])
        outs["fp"].append(fp[:, STATE_PAD - 2:])

        xs, ks_, vs_, cs_, fs_ = _layer(
            xs, bs, ss, l, _pad_state(state_conv[l], conv_hist), _pad_state(state_ffn_conv[l], STATE_PAD),
            cache_k[l].reshape(bs, past, DIFF_W).astype(BF16),
            cache_v[l].reshape(bs, past, DIFF_W).astype(BF16),
            cache_mem_k[l].reshape(bs, mlen, MEM_W).astype(BF16),
            cache_mem_v[l].reshape(bs, mlen, MEM_W).astype(BF16), P, fin, final_norm)
        outs["ks"].append(ks_.reshape(bs, ss, DIFF_H, HEAD_W))
        outs["vs"].append(vs_.reshape(bs, ss, DIFF_H, HEAD_W))
        outs["cs"].append(cs_[:, conv_hist - (ktaps - 1):])
        outs["fs"].append(fs_[:, STATE_PAD - 2:])

    st = lambda n: jnp.stack(outs[n])
    return (xp.reshape(bp, sp, d), xs.reshape(bs, ss, d),
            st("kp"), st("vp"), st("mkp"), st("mvp"), st("cp"), st("fp"),
            st("ks"), st("vs"), st("cs"), st("fs"))
```

```python
import functools
import math

import jax
import jax.numpy as jnp
from jax import lax
from jax.experimental import pallas as pl
from jax.experimental.pallas import tpu as pltpu

F32 = jnp.float32
BF16 = jnp.bfloat16

EPS = 1e-6
CHUNK = 64
DIFF_H = 8
DIFF_DH = 64
HEAD_W = 2 * DIFF_DH
DIFF_W = DIFF_H * HEAD_W
MEM_H = 4
MEM_DH = 128
MEM_W = MEM_H * MEM_DH
N_BRANCH = 3
STATE_PAD = 8

VMEM_LIMIT_BYTES = 56 * 1024 * 1024


def _lambda_init(layer_idx):
    return 0.8 - 0.6 * math.exp(-0.3 * layer_idx)


def _params(*sem):
    return pltpu.CompilerParams(dimension_semantics=sem, vmem_limit_bytes=VMEM_LIMIT_BYTES)


def _pick(n, prefs):
    for p in prefs:
        if n % p == 0:
            return p
    return n


def _rms_rows(x, g):
    return x * lax.rsqrt(jnp.mean(x * x, axis=-1, keepdims=True) + EPS) * g


def _kv_proj_kernel(x_ref, g_ref, w_ref, k_ref, v_ref, kb_ref, vb_ref):
    h = _rms_rows(x_ref[...], g_ref[...]).astype(BF16)
    kv = jnp.dot(h, w_ref[...], preferred_element_type=F32)
    k = kv[:, :DIFF_W]
    v = kv[:, DIFF_W:]
    k_ref[...] = k
    v_ref[...] = v
    kb_ref[...] = k.astype(BF16)
    vb_ref[...] = v.astype(BF16)


def _kv_proj(x, g, w_kv):
    t, d = x.shape
    tm = _pick(t, (512, 256))
    row = lambda i: (i, 0)
    const = lambda i: (0, 0)
    out_blk = pl.BlockSpec((tm, DIFF_W), row)
    return pl.pallas_call(
        _kv_proj_kernel,
        grid=(t // tm,),
        in_specs=[pl.BlockSpec((tm, d), row), pl.BlockSpec((1, d), const),
                  pl.BlockSpec((d, 2 * DIFF_W), const)],
        out_specs=[out_blk, out_blk, out_blk, out_blk],
        out_shape=[jax.ShapeDtypeStruct((t, DIFF_W), F32), jax.ShapeDtypeStruct((t, DIFF_W), F32),
                   jax.ShapeDtypeStruct((t, DIFF_W), BF16), jax.ShapeDtypeStruct((t, DIFF_W), BF16)],
        compiler_params=_params("parallel"),
        name="kv_proj",
    )(x, g, w_kv)


ATT_BLK = 512
VT_ROWS = HEAD_W + 16


def _kv_proj_vt_kernel(x_ref, g_ref, w_ref, k_ref, v_ref, kb_ref, vt_ref):
    h = _rms_rows(x_ref[...], g_ref[...]).astype(BF16)
    kv = jnp.dot(h, w_ref[...], preferred_element_type=F32)
    k = kv[:, :DIFF_W]
    v = kv[:, DIFF_W:]
    k_ref[...] = k
    v_ref[...] = v
    kb_ref[...] = k.astype(BF16)
    tm = v.shape[0]
    for hd in range(DIFF_H):
        vt_ref[hd, 0:HEAD_W, :] = v[:, hd * HEAD_W:(hd + 1) * HEAD_W].T.astype(BF16)
        vt_ref[hd, HEAD_W:VT_ROWS, :] = jnp.ones((VT_ROWS - HEAD_W, tm), BF16)


def _kv_proj_vt(x, g, w_kv, bsz, seq):
    t, d = x.shape
    tm = ATT_BLK
    nb = seq // tm
    row = lambda i: (i, 0)
    const = lambda i: (0, 0)
    out_blk = pl.BlockSpec((tm, DIFF_W), row)
    return pl.pallas_call(
        _kv_proj_vt_kernel,
        grid=(t // tm,),
        in_specs=[pl.BlockSpec((tm, d), row), pl.BlockSpec((1, d), const),
                  pl.BlockSpec((d, 2 * DIFF_W), const)],
        out_specs=[out_blk, out_blk, out_blk,
                   pl.BlockSpec((None, DIFF_H, None, VT_ROWS, tm),
                                lambda i: (i // nb, 0, i % nb, 0, 0))],
        out_shape=[jax.ShapeDtypeStruct((t, DIFF_W), F32), jax.ShapeDtypeStruct((t, DIFF_W), F32),
                   jax.ShapeDtypeStruct((t, DIFF_W), BF16),
                   jax.ShapeDtypeStruct((bsz, DIFF_H, nb, VT_ROWS, tm), BF16)],
        compiler_params=_params("parallel"),
        name="kv_proj_vt",
    )(x, g, w_kv)


def _norm_proj_kernel(x_ref, g_ref, w_ref, o_ref, h_ref):
    @pl.when(pl.program_id(1) == 0)
    def _():
        h_ref[...] = _rms_rows(x_ref[...], g_ref[...]).astype(BF16)

    o_ref[...] = jnp.dot(h_ref[...], w_ref[...], preferred_element_type=F32).astype(o_ref.dtype)


def _norm_proj(x, g, w, out_dtype):
    t, d = x.shape
    n = w.shape[1]
    tm = _pick(t, (1024, 512, 256))
    tn = _pick(n, (512, 256, 128))
    return pl.pallas_call(
        _norm_proj_kernel,
        grid=(t // tm, n // tn),
        in_specs=[pl.BlockSpec((tm, d), lambda i, j: (i, 0)),
                  pl.BlockSpec((1, d), lambda i, j: (0, 0)),
                  pl.BlockSpec((d, tn), lambda i, j: (0, j))],
        out_specs=pl.BlockSpec((tm, tn), lambda i, j: (i, j)),
        out_shape=jax.ShapeDtypeStruct((t, n), out_dtype),
        scratch_shapes=[pltpu.VMEM((tm, d), BF16)],
        compiler_params=_params("parallel", "arbitrary"),
        name="norm_proj",
    )(x, g, w)


CONV_ROWS = 32


def _conv_branch_kernel(cin_ref, ctx_ref, w_ref, b_ref, lg_ref, lb_ref, c_ref, st_ref, ubuf,
                        *, tm, cc, ktaps, hist):
    s = pl.program_id(1)
    lead = hist - (ktaps - 1)

    @pl.when(s == 0)
    def _():
        ubuf[0:hist, :] = ctx_ref[...]

    cin = cin_ref[...].astype(F32)
    u = cin[:, :cc] * jax.nn.sigmoid(cin[:, cc:])
    ubuf[hist:hist + tm, :] = u

    bias = b_ref[...]
    lg = lg_ref[...]
    lb = lb_ref[...]
    for r0 in range(0, tm, CONV_ROWS):
        acc = jnp.zeros((CONV_ROWS, cc), F32) + bias
        for k in range(ktaps):
            a = lead + k + r0
            acc = acc + w_ref[k:k + 1, :] * ubuf[a:a + CONV_ROWS, :]
        mu = jnp.mean(acc, axis=-1, keepdims=True)
        xc = acc - mu
        var = jnp.mean(xc * xc, axis=-1, keepdims=True)
        y = xc * lax.rsqrt(var + EPS) * lg + lb
        c_ref[r0:r0 + CONV_ROWS, :] = (y * jax.nn.sigmoid(y)).astype(c_ref.dtype)

    tail = ubuf[tm:tm + hist, :]
    st_ref[...] = tail
    ubuf[0:hist, :] = tail


def _conv_branch(slab, col_blk, ctx, dw_w, dw_b, ln_g, ln_b, bsz, seq):
    ktaps, cc = dw_w.shape
    hist = 32
    assert ktaps - 1 <= hist and ctx.shape == (bsz, hist, cc)
    tm = _pick(seq, (512, 256, 128, 64, 32))
    nt = seq // tm
    kern = functools.partial(_conv_branch_kernel, tm=tm, cc=cc, ktaps=ktaps, hist=hist)
    vec = pl.BlockSpec((1, cc), lambda b, s: (0, 0))
    return pl.pallas_call(
        kern,
        grid=(bsz, nt),
        in_specs=[pl.BlockSpec((tm, 2 * cc), lambda b, s: (b * nt + s, col_blk)),
                  pl.BlockSpec((None, hist, cc), lambda b, s: (b, 0, 0)),
                  pl.BlockSpec((ktaps, cc), lambda b, s: (0, 0)),
                  vec, vec, vec],
        out_specs=[pl.BlockSpec((tm, cc), lambda b, s: (b * nt + s, 0)),
                   pl.BlockSpec((None, hist, cc), lambda b, s: (b, 0, 0))],
        out_shape=[jax.ShapeDtypeStruct((bsz * seq, cc), BF16),
                   jax.ShapeDtypeStruct((bsz, hist, cc), F32)],
        scratch_shapes=[pltpu.VMEM((hist + tm, cc), F32)],
        compiler_params=_params("arbitrary", "arbitrary"),
        name="conv_branch",
    )(slab, ctx, dw_w, dw_b, ln_g, ln_b)


def _full_attn_kernel(q_ref, k_ref, v_ref, lq1_ref, lk1_ref, lq2_ref, lk2_ref, g_ref, o_ref,
                      qq_ref, *, bq, l_init):
    lane = lax.broadcasted_iota(jnp.int32, (bq, HEAD_W), 1)
    q = q_ref[...] * (DIFF_DH ** -0.5)
    zero = jnp.zeros_like(q)
    qq_ref[0:bq, :] = jnp.where(lane < DIFF_DH, q, zero)
    qq_ref[bq:2 * bq, :] = jnp.where(lane >= DIFF_DH, q, zero)
    s = lax.dot_general(qq_ref[...], k_ref[...], (((1,), (1,)), ((), ())),
                        preferred_element_type=F32)
    p = jnp.exp(s - jnp.max(s, axis=-1, keepdims=True))
    l = jnp.sum(p, axis=-1, keepdims=True)
    o_all = jnp.dot(p.astype(BF16), v_ref[...], preferred_element_type=F32) / l
    lam = (jnp.exp(jnp.sum(lq1_ref[...] * lk1_ref[...], axis=-1, keepdims=True))
           - jnp.exp(jnp.sum(lq2_ref[...] * lk2_ref[...], axis=-1, keepdims=True)) + l_init)
    o = o_all[0:bq, :] - lam * o_all[bq:2 * bq, :]
    o_ref[...] = (_rms_rows(o, g_ref[...]) * (1.0 - l_init)).astype(o_ref.dtype)


def _causal_attn_kernel(q_ref, k_ref, vt_ref, lq1_ref, lk1_ref, lq2_ref, lk2_ref, g_ref, o_ref,
                        qq_ref, sa_ref, sb_ref, ma_ref, mb_ref, m_ref, acc_ref, *, blk, l_init):
    i = pl.program_id(2)
    lane = lax.broadcasted_iota(jnp.int32, (blk, HEAD_W), 1)
    q = q_ref[...] * (DIFF_DH ** -0.5)
    zero = jnp.zeros_like(q)
    qq_ref[0:blk, :] = jnp.where(lane < DIFF_DH, q, zero)
    qq_ref[blk:2 * blk, :] = jnp.where(lane >= DIFF_DH, q, zero)
    m_ref[...] = jnp.full(m_ref.shape, -jnp.inf, F32)
    acc_ref[...] = jnp.zeros(acc_ref.shape, F32)

    def scores(j, s_ref, mx_ref):
        start = pl.multiple_of(j * blk, blk)
        st = lax.dot_general(k_ref[pl.ds(start, blk), :], qq_ref[...], (((1,), (1,)), ((), ())),
                             preferred_element_type=F32)
        s_ref[...] = st
        mx_ref[...] = jnp.max(st, axis=0, keepdims=True)

    def update(j, s_ref, mx_ref, masked):
        st = s_ref[...]
        if masked:
            kchunk = lax.broadcasted_iota(jnp.int32, (blk, 1), 0) // CHUNK
            qcol = lax.broadcasted_iota(jnp.int32, (1, 2 * blk), 1)
            qchunk = jnp.where(qcol >= blk, qcol - blk, qcol) // CHUNK
            st = jnp.where(kchunk <= qchunk, st, -jnp.inf)
            mx = jnp.max(st, axis=0, keepdims=True)
        else:
            mx = mx_ref[...]
        m_prev = m_ref[...]
        m_new = jnp.maximum(m_prev, mx)
        alpha = jnp.exp(m_prev - m_new)
        pt = jnp.exp(st - m_new).astype(BF16)
        acc_ref[...] = alpha * acc_ref[...] + jnp.dot(vt_ref[j], pt, preferred_element_type=F32)
        m_ref[...] = m_new

    scores(0, sa_ref, ma_ref)

    def pair(jp, carry):
        j = 2 * jp
        scores(j + 1, sb_ref, mb_ref)
        update(j, sa_ref, ma_ref, False)
        scores(j + 2, sa_ref, ma_ref)
        update(j + 1, sb_ref, mb_ref, False)
        return carry

    lax.fori_loop(0, i // 2, pair, 0)

    @pl.when(i % 2 == 1)
    def _():
        scores(i, sb_ref, mb_ref)
        update(i - 1, sa_ref, ma_ref, False)
        update(i, sb_ref, mb_ref, True)

    @pl.when(i % 2 == 0)
    def _():
        update(i, sa_ref, ma_ref, True)

    lam = (jnp.exp(jnp.sum(lq1_ref[...] * lk1_ref[...], axis=-1, keepdims=True))
           - jnp.exp(jnp.sum(lq2_ref[...] * lk2_ref[...], axis=-1, keepdims=True)) + l_init)
    acc = acc_ref[...]
    o_all = acc[0:HEAD_W, :] / acc[HEAD_W:HEAD_W + 1, :]
    ot = o_all[:, 0:blk] - lam * o_all[:, blk:2 * blk]
    ot = ot * lax.rsqrt(jnp.mean(ot * ot, axis=0, keepdims=True) + EPS) * (1.0 - l_init)
    o_ref[...] = (ot.T * g_ref[...]).astype(o_ref.dtype)


def _causal_attn(slab, q_col0, kb, vt, lam_p, subln_g, bsz, seq, l_init):
    blk = ATT_BLK
    assert seq % blk == 0 and blk % CHUNK == 0
    nq = seq // blk
    kern = functools.partial(_causal_attn_kernel, blk=blk, l_init=l_init)
    lam_blk = pl.BlockSpec((1, DIFF_DH), lambda b, h, i: (0, 0))
    return pl.pallas_call(
        kern,
        grid=(bsz, DIFF_H, nq),
        in_specs=[pl.BlockSpec((blk, HEAD_W), lambda b, h, i: (b * nq + i, q_col0 + h)),
                  pl.BlockSpec((None, seq, HEAD_W), lambda b, h, i: (b, 0, h)),
                  pl.BlockSpec((None, None, nq, VT_ROWS, blk), lambda b, h, i: (b, h, 0, 0, 0)),
                  lam_blk, lam_blk, lam_blk, lam_blk,
                  pl.BlockSpec((1, HEAD_W), lambda b, h, i: (0, 0))],
        out_specs=pl.BlockSpec((blk, HEAD_W), lambda b, h, i: (b * nq + i, h)),
        out_shape=jax.ShapeDtypeStruct((bsz * seq, DIFF_W), BF16),
        scratch_shapes=[pltpu.VMEM((2 * blk, HEAD_W), BF16),
                        pltpu.VMEM((blk, 2 * blk), F32), pltpu.VMEM((blk, 2 * blk), F32),
                        pltpu.VMEM((1, 2 * blk), F32), pltpu.VMEM((1, 2 * blk), F32),
                        pltpu.VMEM((1, 2 * blk), F32),
                        pltpu.VMEM((VT_ROWS, 2 * blk), F32)],
        compiler_params=_params("parallel", "parallel", "arbitrary"),
        name="causal_attn",
    )(slab, kb, vt, *lam_p, subln_g)


def _full_attn(slab, q_col0, kb, vb, lam_p, subln_g, bsz, sq, l_init):
    sk = kb.shape[1]
    kern = functools.partial(_full_attn_kernel, bq=sq, l_init=l_init)
    lam_blk = pl.BlockSpec((1, DIFF_DH), lambda b, h: (0, 0))
    kv_blk = pl.BlockSpec((None, sk, HEAD_W), lambda b, h: (b, 0, h))
    return pl.pallas_call(
        kern,
        grid=(bsz, DIFF_H),
        in_specs=[pl.BlockSpec((sq, HEAD_W), lambda b, h: (b, q_col0 + h)),
                  kv_blk, kv_blk, lam_blk, lam_blk, lam_blk, lam_blk,
                  pl.BlockSpec((1, HEAD_W), lambda b, h: (0, 0))],
        out_specs=pl.BlockSpec((sq, HEAD_W), lambda b, h: (b, h)),
        out_shape=jax.ShapeDtypeStruct((bsz * sq, DIFF_W), BF16),
        scratch_shapes=[pltpu.VMEM((2 * sq, HEAD_W), BF16)],
        compiler_params=_params("parallel", "parallel"),
        name="full_attn",
    )(slab, kb, vb, *lam_p, subln_g)


def _merge_kernel(x_ref, c_ref, on_ref, mq_ref, gate_ref, mk_ref, mv_ref,
                  wco_ref, wdo_ref, wmo_ref, wout_ref, o_ref, om_ref, *, d):
    scale = MEM_DH ** -0.5
    for h in range(MEM_H):
        cols = slice(h * MEM_DH, (h + 1) * MEM_DH)
        s = lax.dot_general(mq_ref[:, cols], mk_ref[:, cols], (((1,), (1,)), ((), ())),
                            preferred_element_type=F32) * scale
        p = jnp.exp(s - jnp.max(s, axis=-1, keepdims=True))
        p = p / jnp.sum(p, axis=-1, keepdims=True)
        om_ref[:, cols] = jnp.dot(p.astype(BF16), mv_ref[:, cols],
                                  preferred_element_type=F32).astype(BF16)

    def gate(b):
        return jax.nn.sigmoid(gate_ref[:, b * d:(b + 1) * d].astype(F32))

    mix = gate(0) * jnp.dot(c_ref[...], wco_ref[...], preferred_element_type=F32)
    mix = mix + gate(1) * jnp.dot(on_ref[...], wdo_ref[...], preferred_element_type=F32)
    mix = mix + gate(2) * jnp.dot(om_ref[...], wmo_ref[...], preferred_element_type=F32)
    o_ref[...] = x_ref[...] + jnp.dot(mix.astype(BF16), wout_ref[...], preferred_element_type=F32)


def _merge(x, c, on, slab, mq_blk, mk, mv, wco, wdo, wmo, wout, bsz, seq):
    t, d = x.shape
    cc = c.shape[1]
    mlen = mk.shape[1]
    tm = _pick(seq, (256, 128, 64, 32))
    nt = seq // tm
    row = lambda i: (i, 0)
    full = lambda a: pl.BlockSpec(a.shape, lambda i: (0, 0))
    mem_blk = pl.BlockSpec((None, mlen, MEM_W), lambda i: (i // nt, 0, 0))
    return pl.pallas_call(
        functools.partial(_merge_kernel, d=d),
        grid=(t // tm,),
        in_specs=[pl.BlockSpec((tm, d), row), pl.BlockSpec((tm, cc), row),
                  pl.BlockSpec((tm, DIFF_W), row),
                  pl.BlockSpec((tm, MEM_W), lambda i: (i, mq_blk)),
                  pl.BlockSpec((tm, N_BRANCH * d), row),
                  mem_blk, mem_blk, full(wco), full(wdo), full(wmo), full(wout)],
        out_specs=pl.BlockSpec((tm, d), row),
        out_shape=jax.ShapeDtypeStruct((t, d), F32),
        scratch_shapes=[pltpu.VMEM((tm, MEM_W), BF16)],
        compiler_params=_params("parallel"),
        name="merge",
    )(x, c, on, slab, slab, mk, mv, wco, wdo, wmo, wout)


def _ffn_body(x_ref, g_ref, wg_ref, wu_ref, wd_ref, dw_ref, db_ref, ctx_ref, fin_ref,
              o_ref, st_ref, h_ref, acc_ref, ext_ref, tail_ref, *, tm, tf, nt, nf, final_norm):
    i = pl.program_id(0)
    j = pl.program_id(1)
    s = i % nt

    @pl.when(j == 0)
    def _():
        h_ref[...] = _rms_rows(x_ref[...], g_ref[...]).astype(BF16)
        acc_ref[...] = jnp.zeros(acc_ref.shape, F32)

    h = h_ref[...]
    fg = jnp.dot(h, wg_ref[...], preferred_element_type=F32)
    fu = jnp.dot(h, wu_ref[...], preferred_element_type=F32)

    @pl.when(s == 0)
    def _():
        ext_ref[0:STATE_PAD, :] = ctx_ref[...]

    @pl.when(s != 0)
    def _():
        ext_ref[0:STATE_PAD, :] = tail_ref[j]

    ext_ref[STATE_PAD:STATE_PAD + tm, :] = fg
    last = fg[tm - STATE_PAD:tm, :]
    tail_ref[j] = last
    st_ref[:, pl.ds(pl.multiple_of(j * tf, tf), tf)] = last

    y = (dw_ref[0:1, :] * ext_ref[STATE_PAD - 2:STATE_PAD - 2 + tm, :]
         + dw_ref[1:2, :] * ext_ref[STATE_PAD - 1:STATE_PAD - 1 + tm, :]
         + dw_ref[2:3, :] * fg + db_ref[...])
    act = (y * jax.nn.sigmoid(y) * fu).astype(BF16)
    acc_ref[...] += jnp.dot(act, wd_ref[...], preferred_element_type=F32)

    @pl.when(j == nf - 1)
    def _():
        out = x_ref[...] + acc_ref[...]
        if final_norm:
            out = _rms_rows(out, fin_ref[...])
        o_ref[...] = out


def _ffn(x, g, wgu, wd, dw_w, dw_b, ctx, fin_g, bsz, seq, final_norm):
    t, d = x.shape
    ff = wd.shape[0]
    assert dw_w.shape[0] == 3 and ctx.shape == (bsz, STATE_PAD, ff)
    tm = _pick(seq, (512, 256, 128, 64, 32))
    tf = _pick(ff, (512, 256, 128))
    nt, nf = seq // tm, ff // tf
    kern = functools.partial(_ffn_body, tm=tm, tf=tf, nt=nt, nf=nf, final_norm=final_norm)
    return pl.pallas_call(
        kern,
        grid=(t // tm, nf),
        in_specs=[pl.BlockSpec((tm, d), lambda i, j: (i, 0)),
                  pl.BlockSpec((1, d), lambda i, j: (0, 0)),
                  pl.BlockSpec((d, tf), lambda i, j: (0, j)),
                  pl.BlockSpec((d, tf), lambda i, j: (0, nf + j)),
                  pl.BlockSpec((tf, d), lambda i, j: (j, 0)),
                  pl.BlockSpec((3, tf), lambda i, j: (0, j)),
                  pl.BlockSpec((1, tf), lambda i, j: (0, j)),
                  pl.BlockSpec((None, STATE_PAD, tf), lambda i, j: (i // nt, 0, j)),
                  pl.BlockSpec((1, d), lambda i, j: (0, 0))],
        out_specs=[pl.BlockSpec((tm, d), lambda i, j: (i, 0)),
                   pl.BlockSpec((None, STATE_PAD, ff), lambda i, j: (i // nt, 0, 0))],
        out_shape=[jax.ShapeDtypeStruct((t, d), F32),
                   jax.ShapeDtypeStruct((bsz, STATE_PAD, ff), F32)],
        scratch_shapes=[pltpu.VMEM((tm, d), BF16), pltpu.VMEM((tm, d), F32),
                        pltpu.VMEM((STATE_PAD + tm, tf), F32),
                        pltpu.VMEM((nf, STATE_PAD, tf), F32)],
        compiler_params=_params("arbitrary", "arbitrary"),
        name="conv_ffn",
    )(x, g, wgu, wgu, wd, dw_w, dw_b, ctx, fin_g)


def _row(v):
    return v.reshape(1, -1)


def _pad_state(st, rows):
    return jnp.pad(st, ((0, 0), (rows - st.shape[1], 0), (0, 0)))


def _layer(x, bsz, seq, layer_idx, conv_ctx, ffn_ctx, k_past, v_past, mk, mv, P, final_g, final_norm):
    t, d = x.shape
    cc = P["conv_dw_w"].shape[1]
    l_init = _lambda_init(layer_idx)

    slab = _norm_proj(x, P["norm1_g"], P["w_main"], BF16)
    gate_w = N_BRANCH * d
    conv_blk = gate_w // (2 * cc)
    q_col0 = (gate_w + 2 * cc) // HEAD_W
    mq_blk = (gate_w + 2 * cc + DIFF_W) // MEM_W
    assert gate_w % (2 * cc) == 0 and (gate_w + 2 * cc + DIFF_W) % MEM_W == 0

    c, conv_state = _conv_branch(slab, conv_blk, conv_ctx, P["conv_dw_w"], P["conv_dw_b"],
                                 P["conv_ln_g"], P["conv_ln_b"], bsz, seq)

    if k_past is None:
        k, v, kb, vt = _kv_proj_vt(x, P["norm1_g"], P["w_kv"], bsz, seq)
        on = _causal_attn(slab, q_col0, kb.reshape(bsz, seq, DIFF_W), vt, P["lam"],
                          P["diff_subln_g"], bsz, seq, l_init)
    else:
        k, v, kb, vb = _kv_proj(x, P["norm1_g"], P["w_kv"])
        kb3 = jnp.concatenate([k_past, kb.reshape(bsz, seq, DIFF_W)], axis=1)
        vb3 = jnp.concatenate([v_past, vb.reshape(bsz, seq, DIFF_W)], axis=1)
        on = _full_attn(slab, q_col0, kb3, vb3, P["lam"], P["diff_subln_g"], bsz, seq, l_init)

    x1 = _merge(x, c, on, slab, mq_blk, mk, mv, P["w_conv_out"], P["w_diff_out"],
                P["w_mem_out"], P["w_out"], bsz, seq)
    x2, ffn_state = _ffn(x1, P["norm2_g"], P["w_ffn_gu"], P["w_ffn_down"], P["ffn_dw_w"],
                         P["ffn_dw_b"], ffn_ctx, final_g, bsz, seq, final_norm)
    return x2, k, v, conv_state, ffn_state


def kernel(x_prompt, x_sample, mem_prompt, cache_k, cache_v, cache_mem_k, cache_mem_v, state_conv, state_ffn_conv, norm1_g, w_in, conv_dw_w, conv_dw_b, conv_ln_g, conv_ln_b, w_conv_out, lam_q1, lam_k1, lam_q2, lam_k2, diff_subln_g, w_diff_out, mem_norm_g, w_mem_kv, w_mem_out, w_out, norm2_g, w_ffn_gu, ffn_dw_w, ffn_dw_b, w_ffn_down, final_g):
    depth = w_in.shape[0]
    bp, sp, d = x_prompt.shape
    bs, ss, _ = x_sample.shape
    mlen = mem_prompt.shape[1]
    ktaps, cc = conv_dw_w.shape[1:]
    ff = w_ffn_down.shape[1]
    past = cache_k.shape[2]
    conv_hist = 32

    xp = x_prompt.reshape(bp * sp, d)
    xs = x_sample.reshape(bs * ss, d)
    mem = mem_prompt.reshape(bp * mlen, d)
    fin = _row(final_g)

    c0, c1, c2, c3, c4 = (2 * cc, 2 * cc + DIFF_W, 2 * cc + 2 * DIFF_W, 2 * cc + 3 * DIFF_W,
                          2 * cc + 3 * DIFF_W + MEM_W)
    outs = {n: [] for n in ("kp", "vp", "mkp", "mvp", "cp", "fp", "ks", "vs", "cs", "fs")}
    for l in range(depth):
        wl = w_in[l].astype(BF16)
        P = {
            "norm1_g": _row(norm1_g[l]),
            "w_kv": wl[:, c1:c3],
            "w_main": jnp.concatenate([wl[:, c4:], wl[:, :c0], wl[:, c0:c1], wl[:, c3:c4]], axis=1),
            "conv_dw_w": conv_dw_w[l], "conv_dw_b": _row(conv_dw_b[l]),
            "conv_ln_g": _row(conv_ln_g[l]), "conv_ln_b": _row(conv_ln_b[l]),
            "w_conv_out": w_conv_out[l].astype(BF16),
            "lam": (_row(lam_q1[l]), _row(lam_k1[l]), _row(lam_q2[l]), _row(lam_k2[l])),
            "diff_subln_g": _row(diff_subln_g[l]),
            "w_diff_out": w_diff_out[l].astype(BF16),
            "w_mem_out": w_mem_out[l].astype(BF16),
            "w_out": w_out[l].astype(BF16),
            "norm2_g": _row(norm2_g[l]),
            "w_ffn_gu": w_ffn_gu[l].astype(BF16),
            "ffn_dw_w": ffn_dw_w[l], "ffn_dw_b": _row(ffn_dw_b[l]),
            "w_ffn_down": w_ffn_down[l].astype(BF16),
        }
        final_norm = l == depth - 1

        mkv = _norm_proj(mem, _row(mem_norm_g[l]), w_mem_kv[l].astype(BF16), F32)
        mk_p = mkv[:, :MEM_W].reshape(bp, mlen, MEM_W)
        mv_p = mkv[:, MEM_W:].reshape(bp, mlen, MEM_W)
        xp, kp, vp, cp, fp = _layer(
            xp, bp, sp, l, jnp.zeros((bp, conv_hist, cc), F32), jnp.zeros((bp, STATE_PAD, ff), F32),
            None, None, mk_p.astype(BF16), mv_p.astype(BF16), P, fin, final_norm)
        outs["kp"].append(kp.reshape(bp, sp, DIFF_H, HEAD_W))
        outs["vp"].append(vp.reshape(bp, sp, DIFF_H, HEAD_W))
        outs["mkp"].append(mk_p.reshape(bp, mlen, MEM_H, MEM_DH))
        outs["mvp"].append(mv_p.reshape(bp, mlen, MEM_H, MEM_DH))
        outs["cp"].append(cp[:, conv_hist - (ktaps - 1):])
        outs["fp"].append(fp[:, STATE_PAD - 2:])

        xs, ks_, vs_, cs_, fs_ = _layer(
            xs, bs, ss, l, _pad_state(state_conv[l], conv_hist), _pad_state(state_ffn_conv[l], STATE_PAD),
            cache_k[l].reshape(bs, past, DIFF_W).astype(BF16),
            cache_v[l].reshape(bs, past, DIFF_W).astype(BF16),
            cache_mem_k[l].reshape(bs, mlen, MEM_W).astype(BF16),
            cache_mem_v[l].reshape(bs, mlen, MEM_W).astype(BF16), P, fin, final_norm)
        outs["ks"].append(ks_.reshape(bs, ss, DIFF_H, HEAD_W))
        outs["vs"].append(vs_.reshape(bs, ss, DIFF_H, HEAD_W))
        outs["cs"].append(cs_[:, conv_hist - (ktaps - 1):])
        outs["fs"].append(fs_[:, STATE_PAD - 2:])

    st = lambda n: jnp.stack(outs[n])
    return (xp.reshape(bp, sp, d), xs.reshape(bs, ss, d),
            st("kp"), st("vp"), st("mkp"), st("mvp"), st("cp"), st("fp"),
            st("ks"), st("vs"), st("cs"), st("fs"))
```

```python
import functools
import math

import jax
import jax.numpy as jnp
from jax import lax
from jax.experimental import pallas as pl
from jax.experimental.pallas import tpu as pltpu

F32 = jnp.float32
BF16 = jnp.bfloat16

EPS = 1e-6
CHUNK = 64
DIFF_H = 8
DIFF_DH = 64
HEAD_W = 2 * DIFF_DH
DIFF_W = DIFF_H * HEAD_W
MEM_H = 4
MEM_DH = 128
MEM_W = MEM_H * MEM_DH
N_BRANCH = 3
STATE_PAD = 8

VMEM_LIMIT_BYTES = 56 * 1024 * 1024


def _lambda_init(layer_idx):
    return 0.8 - 0.6 * math.exp(-0.3 * layer_idx)


def _params(*sem):
    return pltpu.CompilerParams(dimension_semantics=sem, vmem_limit_bytes=VMEM_LIMIT_BYTES)


def _pick(n, prefs):
    for p in prefs:
        if n % p == 0:
            return p
    return n


def _rms_rows(x, g):
    return x * lax.rsqrt(jnp.mean(x * x, axis=-1, keepdims=True) + EPS) * g


def _kv_proj_kernel(x_ref, g_ref, w_ref, k_ref, v_ref, kb_ref, vb_ref):
    h = _rms_rows(x_ref[...], g_ref[...]).astype(BF16)
    kv = jnp.dot(h, w_ref[...], preferred_element_type=F32)
    k = kv[:, :DIFF_W]
    v = kv[:, DIFF_W:]
    k_ref[...] = k
    v_ref[...] = v
    kb_ref[...] = k.astype(BF16)
    vb_ref[...] = v.astype(BF16)


def _kv_proj(x, g, w_kv):
    t, d = x.shape
    tm = _pick(t, (512, 256))
    row = lambda i: (i, 0)
    const = lambda i: (0, 0)
    out_blk = pl.BlockSpec((tm, DIFF_W), row)
    return pl.pallas_call(
        _kv_proj_kernel,
        grid=(t // tm,),
        in_specs=[pl.BlockSpec((tm, d), row), pl.BlockSpec((1, d), const),
                  pl.BlockSpec((d, 2 * DIFF_W), const)],
        out_specs=[out_blk, out_blk, out_blk, out_blk],
        out_shape=[jax.ShapeDtypeStruct((t, DIFF_W), F32), jax.ShapeDtypeStruct((t, DIFF_W), F32),
                   jax.ShapeDtypeStruct((t, DIFF_W), BF16), jax.ShapeDtypeStruct((t, DIFF_W), BF16)],
        compiler_params=_params("parallel"),
        name="kv_proj",
    )(x, g, w_kv)


ATT_BLK = 512
VT_ROWS = HEAD_W + 16


def _kv_proj_vt_kernel(x_ref, g_ref, w_ref, k_ref, v_ref, kb_ref, vt_ref):
    h = _rms_rows(x_ref[...], g_ref[...]).astype(BF16)
    kv = jnp.dot(h, w_ref[...], preferred_element_type=F32)
    k = kv[:, :DIFF_W]
    v = kv[:, DIFF_W:]
    k_ref[...] = k
    v_ref[...] = v
    kb_ref[...] = k.astype(BF16)
    tm = v.shape[0]
    for hd in range(DIFF_H):
        vt_ref[hd, 0:HEAD_W, :] = v[:, hd * HEAD_W:(hd + 1) * HEAD_W].T.astype(BF16)
        vt_ref[hd, HEAD_W:VT_ROWS, :] = jnp.ones((VT_ROWS - HEAD_W, tm), BF16)


def _kv_proj_vt(x, g, w_kv, bsz, seq):
    t, d = x.shape
    tm = ATT_BLK
    nb = seq // tm
    row = lambda i: (i, 0)
    const = lambda i: (0, 0)
    out_blk = pl.BlockSpec((tm, DIFF_W), row)
    return pl.pallas_call(
        _kv_proj_vt_kernel,
        grid=(t // tm,),
        in_specs=[pl.BlockSpec((tm, d), row), pl.BlockSpec((1, d), const),
                  pl.BlockSpec((d, 2 * DIFF_W), const)],
        out_specs=[out_blk, out_blk, out_blk,
                   pl.BlockSpec((None, DIFF_H, None, VT_ROWS, tm),
                                lambda i: (i // nb, 0, i % nb, 0, 0))],
        out_shape=[jax.ShapeDtypeStruct((t, DIFF_W), F32), jax.ShapeDtypeStruct((t, DIFF_W), F32),
                   jax.ShapeDtypeStruct((t, DIFF_W), BF16),
                   jax.ShapeDtypeStruct((bsz, DIFF_H, nb, VT_ROWS, tm), BF16)],
        compiler_params=_params("parallel"),
        name="kv_proj_vt",
    )(x, g, w_kv)


def _norm_proj_kernel(x_ref, g_ref, w_ref, o_ref, h_ref):
    @pl.when(pl.program_id(1) == 0)
    def _():
        h_ref[...] = _rms_rows(x_ref[...], g_ref[...]).astype(BF16)

    o_ref[...] = jnp.dot(h_ref[...], w_ref[...], preferred_element_type=F32).astype(o_ref.dtype)


def _norm_proj(x, g, w, out_dtype):
    t, d = x.shape
    n = w.shape[1]
    tm = _pick(t, (1024, 512, 256))
    tn = _pick(n, (512, 256, 128))
    return pl.pallas_call(
        _norm_proj_kernel,
        grid=(t // tm, n // tn),
        in_specs=[pl.BlockSpec((tm, d), lambda i, j: (i, 0)),
                  pl.BlockSpec((1, d), lambda i, j: (0, 0)),
                  pl.BlockSpec((d, tn), lambda i, j: (0, j))],
        out_specs=pl.BlockSpec((tm, tn), lambda i, j: (i, j)),
        out_shape=jax.ShapeDtypeStruct((t, n), out_dtype),
        scratch_shapes=[pltpu.VMEM((tm, d), BF16)],
        compiler_params=_params("parallel", "arbitrary"),
        name="norm_proj",
    )(x, g, w)


CONV_ROWS = 32


def _conv_branch_kernel(cin_ref, ctx_ref, w_ref, b_ref, lg_ref, lb_ref, c_ref, st_ref, ubuf,
                        *, tm, cc, ktaps, hist):
    s = pl.program_id(1)
    lead = hist - (ktaps - 1)

    @pl.when(s == 0)
    def _():
        ubuf[0:hist, :] = ctx_ref[...]

    cin = cin_ref[...].astype(F32)
    u = cin[:, :cc] * jax.nn.sigmoid(cin[:, cc:])
    ubuf[hist:hist + tm, :] = u

    bias = b_ref[...]
    lg = lg_ref[...]
    lb = lb_ref[...]
    for r0 in range(0, tm, CONV_ROWS):
        acc = jnp.zeros((CONV_ROWS, cc), F32) + bias
        for k in range(ktaps):
            a = lead + k + r0
            acc = acc + w_ref[k:k + 1, :] * ubuf[a:a + CONV_ROWS, :]
        mu = jnp.mean(acc, axis=-1, keepdims=True)
        xc = acc - mu
        var = jnp.mean(xc * xc, axis=-1, keepdims=True)
        y = xc * lax.rsqrt(var + EPS) * lg + lb
        c_ref[r0:r0 + CONV_ROWS, :] = (y * jax.nn.sigmoid(y)).astype(c_ref.dtype)

    tail = ubuf[tm:tm + hist, :]
    st_ref[...] = tail
    ubuf[0:hist, :] = tail


def _conv_branch(slab, col_blk, ctx, dw_w, dw_b, ln_g, ln_b, bsz, seq):
    ktaps, cc = dw_w.shape
    hist = 32
    assert ktaps - 1 <= hist and ctx.shape == (bsz, hist, cc)
    tm = _pick(seq, (512, 256, 128, 64, 32))
    nt = seq // tm
    kern = functools.partial(_conv_branch_kernel, tm=tm, cc=cc, ktaps=ktaps, hist=hist)
    vec = pl.BlockSpec((1, cc), lambda b, s: (0, 0))
    return pl.pallas_call(
        kern,
        grid=(bsz, nt),
        in_specs=[pl.BlockSpec((tm, 2 * cc), lambda b, s: (b * nt + s, col_blk)),
                  pl.BlockSpec((None, hist, cc), lambda b, s: (b, 0, 0)),
                  pl.BlockSpec((ktaps, cc), lambda b, s: (0, 0)),
                  vec, vec, vec],
        out_specs=[pl.BlockSpec((tm, cc), lambda b, s: (b * nt + s, 0)),
                   pl.BlockSpec((None, hist, cc), lambda b, s: (b, 0, 0))],
        out_shape=[jax.ShapeDtypeStruct((bsz * seq, cc), BF16),
                   jax.ShapeDtypeStruct((bsz, hist, cc), F32)],
        scratch_shapes=[pltpu.VMEM((hist + tm, cc), F32)],
        compiler_params=_params("arbitrary", "arbitrary"),
        name="conv_branch",
    )(slab, ctx, dw_w, dw_b, ln_g, ln_b)


def _full_attn_kernel(q_ref, kp_ref, vp_ref, kn_ref, vn_ref, lq1_ref, lk1_ref, lq2_ref, lk2_ref,
                      g_ref, o_ref, qq_ref, *, bq, l_init):
    lane = lax.broadcasted_iota(jnp.int32, (bq, HEAD_W), 1)
    q = q_ref[...] * (DIFF_DH ** -0.5)
    zero = jnp.zeros_like(q)
    qq_ref[0:bq, :] = jnp.where(lane < DIFF_DH, q, zero)
    qq_ref[bq:2 * bq, :] = jnp.where(lane >= DIFF_DH, q, zero)
    nt_dims = (((1,), (1,)), ((), ()))
    qq = qq_ref[...]
    s_p = lax.dot_general(qq, kp_ref[...].astype(BF16), nt_dims, preferred_element_type=F32)
    s_n = lax.dot_general(qq, kn_ref[...], nt_dims, preferred_element_type=F32)
    m = jnp.maximum(jnp.max(s_p, axis=-1, keepdims=True), jnp.max(s_n, axis=-1, keepdims=True))
    p_p = jnp.exp(s_p - m)
    p_n = jnp.exp(s_n - m)
    l = jnp.sum(p_p, axis=-1, keepdims=True) + jnp.sum(p_n, axis=-1, keepdims=True)
    o_all = (jnp.dot(p_p.astype(BF16), vp_ref[...].astype(BF16), preferred_element_type=F32)
             + jnp.dot(p_n.astype(BF16), vn_ref[...], preferred_element_type=F32)) / l
    lam = (jnp.exp(jnp.sum(lq1_ref[...] * lk1_ref[...], axis=-1, keepdims=True))
           - jnp.exp(jnp.sum(lq2_ref[...] * lk2_ref[...], axis=-1, keepdims=True)) + l_init)
    o = o_all[0:bq, :] - lam * o_all[bq:2 * bq, :]
    o_ref[...] = (_rms_rows(o, g_ref[...]) * (1.0 - l_init)).astype(o_ref.dtype)


def _causal_attn_kernel(q_ref, k_ref, vt_ref, lq1_ref, lk1_ref, lq2_ref, lk2_ref, g_ref, o_ref,
                        qq_ref, sa_ref, sb_ref, ma_ref, mb_ref, m_ref, acc_ref, *, blk, l_init):
    i = pl.program_id(2)
    lane = lax.broadcasted_iota(jnp.int32, (blk, HEAD_W), 1)
    q = q_ref[...] * (DIFF_DH ** -0.5)
    zero = jnp.zeros_like(q)
    qq_ref[0:blk, :] = jnp.where(lane < DIFF_DH, q, zero)
    qq_ref[blk:2 * blk, :] = jnp.where(lane >= DIFF_DH, q, zero)
    m_ref[...] = jnp.full(m_ref.shape, -jnp.inf, F32)
    acc_ref[...] = jnp.zeros(acc_ref.shape, F32)

    def scores(j, s_ref, mx_ref):
        start = pl.multiple_of(j * blk, blk)
        st = lax.dot_general(k_ref[pl.ds(start, blk), :], qq_ref[...], (((1,), (1,)), ((), ())),
                             preferred_element_type=F32)
        s_ref[...] = st
        mx_ref[...] = jnp.max(st, axis=0, keepdims=True)

    def update(j, s_ref, mx_ref, masked):
        st = s_ref[...]
        if masked:
            kchunk = lax.broadcasted_iota(jnp.int32, (blk, 1), 0) // CHUNK
            qcol = lax.broadcasted_iota(jnp.int32, (1, 2 * blk), 1)
            qchunk = jnp.where(qcol >= blk, qcol - blk, qcol) // CHUNK
            st = jnp.where(kchunk <= qchunk, st, -jnp.inf)
            mx = jnp.max(st, axis=0, keepdims=True)
        else:
            mx = mx_ref[...]
        m_prev = m_ref[...]
        m_new = jnp.maximum(m_prev, mx)
        alpha = jnp.exp(m_prev - m_new)
        pt = jnp.exp(st - m_new).astype(BF16)
        acc_ref[...] = alpha * acc_ref[...] + jnp.dot(vt_ref[j], pt, preferred_element_type=F32)
        m_ref[...] = m_new

    scores(0, sa_ref, ma_ref)

    def pair(jp, carry):
        j = 2 * jp
        scores(j + 1, sb_ref, mb_ref)
        update(j, sa_ref, ma_ref, False)
        scores(j + 2, sa_ref, ma_ref)
        update(j + 1, sb_ref, mb_ref, False)
        return carry

    lax.fori_loop(0, i // 2, pair, 0)

    @pl.when(i % 2 == 1)
    def _():
        scores(i, sb_ref, mb_ref)
        update(i - 1, sa_ref, ma_ref, False)
        update(i, sb_ref, mb_ref, True)

    @pl.when(i % 2 == 0)
    def _():
        update(i, sa_ref, ma_ref, True)

    lam = (jnp.exp(jnp.sum(lq1_ref[...] * lk1_ref[...], axis=-1, keepdims=True))
           - jnp.exp(jnp.sum(lq2_ref[...] * lk2_ref[...], axis=-1, keepdims=True)) + l_init)
    acc = acc_ref[...]
    o_all = acc[0:HEAD_W, :] / acc[HEAD_W:HEAD_W + 1, :]
    ot = o_all[:, 0:blk] - lam * o_all[:, blk:2 * blk]
    ot = ot * lax.rsqrt(jnp.mean(ot * ot, axis=0, keepdims=True) + EPS) * (1.0 - l_init)
    o_ref[...] = (ot.T * g_ref[...]).astype(o_ref.dtype)


def _causal_attn(slab, q_col0, kb, vt, lam_p, subln_g, bsz, seq, l_init):
    blk = ATT_BLK
    assert seq % blk == 0 and blk % CHUNK == 0
    nq = seq // blk
    kern = functools.partial(_causal_attn_kernel, blk=blk, l_init=l_init)
    lam_blk = pl.BlockSpec((1, DIFF_DH), lambda b, h, i: (0, 0))
    return pl.pallas_call(
        kern,
        grid=(bsz, DIFF_H, nq),
        in_specs=[pl.BlockSpec((blk, HEAD_W), lambda b, h, i: (b * nq + i, q_col0 + h)),
                  pl.BlockSpec((None, seq, HEAD_W), lambda b, h, i: (b, 0, h)),
                  pl.BlockSpec((None, None, nq, VT_ROWS, blk), lambda b, h, i: (b, h, 0, 0, 0)),
                  lam_blk, lam_blk, lam_blk, lam_blk,
                  pl.BlockSpec((1, HEAD_W), lambda b, h, i: (0, 0))],
        out_specs=pl.BlockSpec((blk, HEAD_W), lambda b, h, i: (b * nq + i, h)),
        out_shape=jax.ShapeDtypeStruct((bsz * seq, DIFF_W), BF16),
        scratch_shapes=[pltpu.VMEM((2 * blk, HEAD_W), BF16),
                        pltpu.VMEM((blk, 2 * blk), F32), pltpu.VMEM((blk, 2 * blk), F32),
                        pltpu.VMEM((1, 2 * blk), F32), pltpu.VMEM((1, 2 * blk), F32),
                        pltpu.VMEM((1, 2 * blk), F32),
                        pltpu.VMEM((VT_ROWS, 2 * blk), F32)],
        compiler_params=_params("parallel", "parallel", "arbitrary"),
        name="causal_attn",
    )(slab, kb, vt, *lam_p, subln_g)


def _full_attn(slab, q_col0, k_past, v_past, kb, vb, lam_p, subln_g, bsz, sq, l_init):
    past = k_past.shape[1]
    kern = functools.partial(_full_attn_kernel, bq=sq, l_init=l_init)
    lam_blk = pl.BlockSpec((1, DIFF_DH), lambda b, h: (0, 0))
    past_blk = pl.BlockSpec((None, past, HEAD_W), lambda b, h: (b, 0, h))
    new_blk = pl.BlockSpec((sq, HEAD_W), lambda b, h: (b, h))
    return pl.pallas_call(
        kern,
        grid=(bsz, DIFF_H),
        in_specs=[pl.BlockSpec((sq, HEAD_W), lambda b, h: (b, q_col0 + h)),
                  past_blk, past_blk, new_blk, new_blk, lam_blk, lam_blk, lam_blk, lam_blk,
                  pl.BlockSpec((1, HEAD_W), lambda b, h: (0, 0))],
        out_specs=pl.BlockSpec((sq, HEAD_W), lambda b, h: (b, h)),
        out_shape=jax.ShapeDtypeStruct((bsz * sq, DIFF_W), BF16),
        scratch_shapes=[pltpu.VMEM((2 * sq, HEAD_W), BF16)],
        compiler_params=_params("parallel", "parallel"),
        name="full_attn",
    )(slab, k_past, v_past, kb, vb, *lam_p, subln_g)


def _merge_kernel(x_ref, c_ref, on_ref, mq_ref, gate_ref, mk_ref, mv_ref,
                  wco_ref, wdo_ref, wmo_ref, wout_ref, o_ref, om_ref, *, d):
    scale = MEM_DH ** -0.5
    for h in range(MEM_H):
        cols = slice(h * MEM_DH, (h + 1) * MEM_DH)
        s = lax.dot_general(mq_ref[:, cols], mk_ref[:, cols], (((1,), (1,)), ((), ())),
                            preferred_element_type=F32) * scale
        p = jnp.exp(s - jnp.max(s, axis=-1, keepdims=True))
        p = p / jnp.sum(p, axis=-1, keepdims=True)
        om_ref[:, cols] = jnp.dot(p.astype(BF16), mv_ref[:, cols],
                                  preferred_element_type=F32).astype(BF16)

    def gate(b):
        return jax.nn.sigmoid(gate_ref[:, b * d:(b + 1) * d].astype(F32))

    mix = gate(0) * jnp.dot(c_ref[...], wco_ref[...], preferred_element_type=F32)
    mix = mix + gate(1) * jnp.dot(on_ref[...], wdo_ref[...], preferred_element_type=F32)
    mix = mix + gate(2) * jnp.dot(om_ref[...], wmo_ref[...], preferred_element_type=F32)
    o_ref[...] = x_ref[...] + jnp.dot(mix.astype(BF16), wout_ref[...], preferred_element_type=F32)


def _merge(x, c, on, slab, mq_blk, mk, mv, wco, wdo, wmo, wout, bsz, seq):
    t, d = x.shape
    cc = c.shape[1]
    mlen = mk.shape[1]
    tm = _pick(seq, (256, 128, 64, 32))
    nt = seq // tm
    row = lambda i: (i, 0)
    full = lambda a: pl.BlockSpec(a.shape, lambda i: (0, 0))
    mem_blk = pl.BlockSpec((None, mlen, MEM_W), lambda i: (i // nt, 0, 0))
    return pl.pallas_call(
        functools.partial(_merge_kernel, d=d),
        grid=(t // tm,),
        in_specs=[pl.BlockSpec((tm, d), row), pl.BlockSpec((tm, cc), row),
                  pl.BlockSpec((tm, DIFF_W), row),
                  pl.BlockSpec((tm, MEM_W), lambda i: (i, mq_blk)),
                  pl.BlockSpec((tm, N_BRANCH * d), row),
                  mem_blk, mem_blk, full(wco), full(wdo), full(wmo), full(wout)],
        out_specs=pl.BlockSpec((tm, d), row),
        out_shape=jax.ShapeDtypeStruct((t, d), F32),
        scratch_shapes=[pltpu.VMEM((tm, MEM_W), BF16)],
        compiler_params=_params("parallel"),
        name="merge",
    )(x, c, on, slab, slab, mk, mv, wco, wdo, wmo, wout)


def _ffn_body(x_ref, g_ref, wg_ref, wu_ref, wd_ref, dw_ref, db_ref, ctx_ref, fin_ref,
              o_ref, st_ref, h_ref, acc_ref, ext_ref, tail_ref, acta_ref, actb_ref,
              *, tm, tf, nt, nf, final_norm):
    i = pl.program_id(0)
    j = pl.program_id(1)
    s = i % nt

    def gate_up(act_ref):
        h = h_ref[...]
        fg = jnp.dot(h, wg_ref[...], preferred_element_type=F32)
        fu = jnp.dot(h, wu_ref[...], preferred_element_type=F32)
        ext_ref[0:STATE_PAD, :] = jnp.where(s == 0, ctx_ref[...], tail_ref[j])
        ext_ref[STATE_PAD:STATE_PAD + tm, :] = fg
        last = fg[tm - STATE_PAD:tm, :]
        tail_ref[j] = last
        st_ref[:, pl.ds(pl.multiple_of(j * tf, tf), tf)] = last
        y = (dw_ref[0:1, :] * ext_ref[STATE_PAD - 2:STATE_PAD - 2 + tm, :]
             + dw_ref[1:2, :] * ext_ref[STATE_PAD - 1:STATE_PAD - 1 + tm, :]
             + dw_ref[2:3, :] * fg + db_ref[...])
        act_ref[...] = (y * jax.nn.sigmoid(y) * fu).astype(BF16)

    def down(act_ref):
        acc_ref[...] += jnp.dot(act_ref[...], wd_ref[...], preferred_element_type=F32)

    slots = (acta_ref, actb_ref)

    @pl.when(j == 0)
    def _():
        @pl.when(i == 0)
        def _():
            tail_ref[...] = jnp.zeros(tail_ref.shape, F32)

        h_ref[...] = _rms_rows(x_ref[...], g_ref[...]).astype(BF16)
        acc_ref[...] = jnp.zeros(acc_ref.shape, F32)
        gate_up(slots[0])

    for parity in (0, 1):
        @pl.when((j > 0) & (j < nf) & (j % 2 == parity))
        def _():
            down(slots[1 - parity])
            gate_up(slots[parity])

    @pl.when(j == nf)
    def _():
        down(slots[(nf - 1) % 2])
        out = x_ref[...] + acc_ref[...]
        if final_norm:
            out = _rms_rows(out, fin_ref[...])
        o_ref[...] = out


def _ffn(x, g, wgu, wd, dw_w, dw_b, ctx, fin_g, bsz, seq, final_norm):
    t, d = x.shape
    ff = wd.shape[0]
    assert dw_w.shape[0] == 3 and ctx.shape == (bsz, STATE_PAD, ff)
    tm = _pick(seq, (512, 256, 128, 64, 32))
    tf = _pick(ff, (512, 256, 128))
    nt, nf = seq // tm, ff // tf
    assert nf >= 2
    kern = functools.partial(_ffn_body, tm=tm, tf=tf, nt=nt, nf=nf, final_norm=final_norm)
    cur = lambda j: jnp.minimum(j, nf - 1)
    prev = lambda j: jnp.maximum(j - 1, 0)
    return pl.pallas_call(
        kern,
        grid=(t // tm, nf + 1),
        in_specs=[pl.BlockSpec((tm, d), lambda i, j: (i, 0)),
                  pl.BlockSpec((1, d), lambda i, j: (0, 0)),
                  pl.BlockSpec((d, tf), lambda i, j: (0, cur(j))),
                  pl.BlockSpec((d, tf), lambda i, j: (0, nf + cur(j))),
                  pl.BlockSpec((tf, d), lambda i, j: (prev(j), 0)),
                  pl.BlockSpec((3, tf), lambda i, j: (0, cur(j))),
                  pl.BlockSpec((1, tf), lambda i, j: (0, cur(j))),
                  pl.BlockSpec((None, STATE_PAD, tf), lambda i, j: (i // nt, 0, cur(j))),
                  pl.BlockSpec((1, d), lambda i, j: (0, 0))],
        out_specs=[pl.BlockSpec((tm, d), lambda i, j: (i, 0)),
                   pl.BlockSpec((None, STATE_PAD, ff), lambda i, j: (i // nt, 0, 0))],
        out_shape=[jax.ShapeDtypeStruct((t, d), F32),
                   jax.ShapeDtypeStruct((bsz, STATE_PAD, ff), F32)],
        scratch_shapes=[pltpu.VMEM((tm, d), BF16), pltpu.VMEM((tm, d), F32),
                        pltpu.VMEM((STATE_PAD + tm, tf), F32),
                        pltpu.VMEM((nf, STATE_PAD, tf), F32),
                        pltpu.VMEM((tm, tf), BF16), pltpu.VMEM((tm, tf), BF16)],
        compiler_params=_params("arbitrary", "arbitrary"),
        name="conv_ffn",
    )(x, g, wgu, wgu, wd, dw_w, dw_b, ctx, fin_g)


def _row(v):
    return v.reshape(1, -1)


def _pad_state(st, rows):
    return jnp.pad(st, ((0, 0), (rows - st.shape[1], 0), (0, 0)))


def _layer(x, bsz, seq, layer_idx, conv_ctx, ffn_ctx, k_past, v_past, mk, mv, P, final_g, final_norm):
    t, d = x.shape
    cc = P["conv_dw_w"].shape[1]
    l_init = _lambda_init(layer_idx)

    slab = _norm_proj(x, P["norm1_g"], P["w_main"], BF16)
    gate_w = N_BRANCH * d
    conv_blk = gate_w // (2 * cc)
    q_col0 = (gate_w + 2 * cc) // HEAD_W
    mq_blk = (gate_w + 2 * cc + DIFF_W) // MEM_W
    assert gate_w % (2 * cc) == 0 and (gate_w + 2 * cc + DIFF_W) % MEM_W == 0

    c, conv_state = _conv_branch(slab, conv_blk, conv_ctx, P["conv_dw_w"], P["conv_dw_b"],
                                 P["conv_ln_g"], P["conv_ln_b"], bsz, seq)

    if k_past is None:
        k, v, kb, vt = _kv_proj_vt(x, P["norm1_g"], P["w_kv"], bsz, seq)
        on = _causal_attn(slab, q_col0, kb.reshape(bsz, seq, DIFF_W), vt, P["lam"],
                          P["diff_subln_g"], bsz, seq, l_init)
    else:
        k, v, kb, vb = _kv_proj(x, P["norm1_g"], P["w_kv"])
        on = _full_attn(slab, q_col0, k_past, v_past, kb, vb, P["lam"], P["diff_subln_g"],
                        bsz, seq, l_init)

    x1 = _merge(x, c, on, slab, mq_blk, mk, mv, P["w_conv_out"], P["w_diff_out"],
                P["w_mem_out"], P["w_out"], bsz, seq)
    x2, ffn_state = _ffn(x1, P["norm2_g"], P["w_ffn_gu"], P["w_ffn_down"], P["ffn_dw_w"],
                         P["ffn_dw_b"], ffn_ctx, final_g, bsz, seq, final_norm)
    return x2, k, v, conv_state, ffn_state


def kernel(x_prompt, x_sample, mem_prompt, cache_k, cache_v, cache_mem_k, cache_mem_v, state_conv, state_ffn_conv, norm1_g, w_in, conv_dw_w, conv_dw_b, conv_ln_g, conv_ln_b, w_conv_out, lam_q1, lam_k1, lam_q2, lam_k2, diff_subln_g, w_diff_out, mem_norm_g, w_mem_kv, w_mem_out, w_out, norm2_g, w_ffn_gu, ffn_dw_w, ffn_dw_b, w_ffn_down, final_g):
    depth = w_in.shape[0]
    bp, sp, d = x_prompt.shape
    bs, ss, _ = x_sample.shape
    mlen = mem_prompt.shape[1]
    ktaps, cc = conv_dw_w.shape[1:]
    ff = w_ffn_down.shape[1]
    past = cache_k.shape[2]
    conv_hist = 32

    xp = x_prompt.reshape(bp * sp, d)
    xs = x_sample.reshape(bs * ss, d)
    mem = mem_prompt.reshape(bp * mlen, d)
    fin = _row(final_g)

    c0, c1, c2, c3, c4 = (2 * cc, 2 * cc + DIFF_W, 2 * cc + 2 * DIFF_W, 2 * cc + 3 * DIFF_W,
                          2 * cc + 3 * DIFF_W + MEM_W)
    outs = {n: [] for n in ("kp", "vp", "mkp", "mvp", "cp", "fp", "ks", "vs", "cs", "fs")}
    for l in range(depth):
        wl = w_in[l].astype(BF16)
        P = {
            "norm1_g": _row(norm1_g[l]),
            "w_kv": wl[:, c1:c3],
            "w_main": jnp.concatenate([wl[:, c4:], wl[:, :c0], wl[:, c0:c1], wl[:, c3:c4]], axis=1),
            "conv_dw_w": conv_dw_w[l], "conv_dw_b": _row(conv_dw_b[l]),
            "conv_ln_g": _row(conv_ln_g[l]), "conv_ln_b": _row(conv_ln_b[l]),
            "w_conv_out": w_conv_out[l].astype(BF16),
            "lam": (_row(lam_q1[l]), _row(lam_k1[l]), _row(lam_q2[l]), _row(lam_k2[l])),
            "diff_subln_g": _row(diff_subln_g[l]),
            "w_diff_out": w_diff_out[l].astype(BF16),
            "w_mem_out": w_mem_out[l].astype(BF16),
            "w_out": w_out[l].astype(BF16),
            "norm2_g": _row(norm2_g[l]),
            "w_ffn_gu": w_ffn_gu[l].astype(BF16),
            "ffn_dw_w": ffn_dw_w[l], "ffn_dw_b": _row(ffn_dw_b[l]),
            "w_ffn_down": w_ffn_down[l].astype(BF16),
        }
        final_norm = l == depth - 1

        mkv = _norm_proj(mem, _row(mem_norm_g[l]), w_mem_kv[l].astype(BF16), F32)
        mk_p = mkv[:, :MEM_W].reshape(bp, mlen, MEM_W)
        mv_p = mkv[:, MEM_W:].reshape(bp, mlen, MEM_W)
        xp, kp, vp, cp, fp = _layer(
            xp, bp, sp, l, jnp.zeros((bp, conv_hist, cc), F32), jnp.zeros((bp, STATE_PAD, ff), F32),
            None, None, mk_p.astype(BF16), mv_p.astype(BF16), P, fin, final_norm)
        outs["kp"].append(kp.reshape(bp, sp, DIFF_H, HEAD_W))
        outs["vp"].append(vp.reshape(bp, sp, DIFF_H, HEAD_W))
        outs["mkp"].append(mk_p.reshape(bp, mlen, MEM_H, MEM_DH))
        outs["mvp"].append(mv_p.reshape(bp, mlen, MEM_H, MEM_DH))
        outs["cp"].append(cp[:, conv_hist - (ktaps - 1):])
        outs["fp"].append(fp[:, STATE_PAD - 2:])

        xs, ks_, vs_, cs_, fs_ = _layer(
            xs, bs, ss, l, _pad_state(state_conv[l], conv_hist), _pad_state(state_ffn_conv[l], STATE_PAD),
            cache_k[l].reshape(bs, past, DIFF_W), cache_v[l].reshape(bs, past, DIFF_W),
            cache_mem_k[l].reshape(bs, mlen, MEM_W).astype(BF16),
            cache_mem_v[l].reshape(bs, mlen, MEM_W).astype(BF16), P, fin, final_norm)
        outs["ks"].append(ks_.reshape(bs, ss, DIFF_H, HEAD_W))
        outs["vs"].append(vs_.reshape(bs, ss, DIFF_H, HEAD_W))
        outs["cs"].append(cs_[:, conv_hist - (ktaps - 1):])
        outs["fs"].append(fs_[:, STATE_PAD - 2:])

    st = lambda n: jnp.stack(outs[n])
    return (xp.reshape(bp, sp, d), xs.reshape(bs, ss, d),
            st("kp"), st("vp"), st("mkp"), st("mvp"), st("cp"), st("fp"),
            st("ks"), st("vs"), st("cs"), st("fs"))
```

```python
import functools
import math

import jax
import jax.numpy as jnp
from jax import lax
from jax.experimental import pallas as pl
from jax.experimental.pallas import tpu as pltpu

F32 = jnp.float32
BF16 = jnp.bfloat16

EPS = 1e-6
CHUNK = 64
DIFF_H = 8
DIFF_DH = 64
HEAD_W = 2 * DIFF_DH
DIFF_W = DIFF_H * HEAD_W
MEM_H = 4
MEM_DH = 128
MEM_W = MEM_H * MEM_DH
N_BRANCH = 3
STATE_PAD = 8

VMEM_LIMIT_BYTES = 56 * 1024 * 1024


def _lambda_init(layer_idx):
    return 0.8 - 0.6 * math.exp(-0.3 * layer_idx)


def _params(*sem):
    return pltpu.CompilerParams(dimension_semantics=sem, vmem_limit_bytes=VMEM_LIMIT_BYTES)


def _pick(n, prefs):
    for p in prefs:
        if n % p == 0:
            return p
    return n


def _rms_rows(x, g):
    return x * lax.rsqrt(jnp.mean(x * x, axis=-1, keepdims=True) + EPS) * g


def _kv_proj_kernel(x_ref, g_ref, w_ref, k_ref, v_ref, kb_ref, vb_ref):
    h = _rms_rows(x_ref[...], g_ref[...]).astype(BF16)
    kv = jnp.dot(h, w_ref[...], preferred_element_type=F32)
    k = kv[:, :DIFF_W]
    v = kv[:, DIFF_W:]
    k_ref[...] = k
    v_ref[...] = v
    kb_ref[...] = k.astype(BF16)
    vb_ref[...] = v.astype(BF16)


def _kv_proj(x, g, w_kv):
    t, d = x.shape
    tm = _pick(t, (512, 256))
    row = lambda i: (i, 0)
    const = lambda i: (0, 0)
    out_blk = pl.BlockSpec((tm, DIFF_W), row)
    return pl.pallas_call(
        _kv_proj_kernel,
        grid=(t // tm,),
        in_specs=[pl.BlockSpec((tm, d), row), pl.BlockSpec((1, d), const),
                  pl.BlockSpec((d, 2 * DIFF_W), const)],
        out_specs=[out_blk, out_blk, out_blk, out_blk],
        out_shape=[jax.ShapeDtypeStruct((t, DIFF_W), F32), jax.ShapeDtypeStruct((t, DIFF_W), F32),
                   jax.ShapeDtypeStruct((t, DIFF_W), BF16), jax.ShapeDtypeStruct((t, DIFF_W), BF16)],
        compiler_params=_params("parallel"),
        name="kv_proj",
    )(x, g, w_kv)


ATT_BLK = 512
VT_ROWS = HEAD_W + 16


def _kv_proj_vt_kernel(x_ref, g_ref, w_ref, k_ref, v_ref, kb_ref, vt_ref):
    h = _rms_rows(x_ref[...], g_ref[...]).astype(BF16)
    kv = jnp.dot(h, w_ref[...], preferred_element_type=F32)
    k = kv[:, :DIFF_W]
    v = kv[:, DIFF_W:]
    k_ref[...] = k
    v_ref[...] = v
    kb_ref[...] = k.astype(BF16)
    tm = v.shape[0]
    for hd in range(DIFF_H):
        vt_ref[hd, 0:HEAD_W, :] = v[:, hd * HEAD_W:(hd + 1) * HEAD_W].T.astype(BF16)
        vt_ref[hd, HEAD_W:VT_ROWS, :] = jnp.ones((VT_ROWS - HEAD_W, tm), BF16)


def _kv_proj_vt(x, g, w_kv, bsz, seq):
    t, d = x.shape
    tm = ATT_BLK
    nb = seq // tm
    row = lambda i: (i, 0)
    const = lambda i: (0, 0)
    out_blk = pl.BlockSpec((tm, DIFF_W), row)
    return pl.pallas_call(
        _kv_proj_vt_kernel,
        grid=(t // tm,),
        in_specs=[pl.BlockSpec((tm, d), row), pl.BlockSpec((1, d), const),
                  pl.BlockSpec((d, 2 * DIFF_W), const)],
        out_specs=[out_blk, out_blk, out_blk,
                   pl.BlockSpec((None, DIFF_H, None, VT_ROWS, tm),
                                lambda i: (i // nb, 0, i % nb, 0, 0))],
        out_shape=[jax.ShapeDtypeStruct((t, DIFF_W), F32), jax.ShapeDtypeStruct((t, DIFF_W), F32),
                   jax.ShapeDtypeStruct((t, DIFF_W), BF16),
                   jax.ShapeDtypeStruct((bsz, DIFF_H, nb, VT_ROWS, tm), BF16)],
        compiler_params=_params("parallel"),
        name="kv_proj_vt",
    )(x, g, w_kv)


def _norm_proj_kernel(x_ref, g_ref, w_ref, o_ref, h_ref):
    @pl.when(pl.program_id(1) == 0)
    def _():
        h_ref[...] = _rms_rows(x_ref[...], g_ref[...]).astype(BF16)

    o_ref[...] = jnp.dot(h_ref[...], w_ref[...], preferred_element_type=F32).astype(o_ref.dtype)


def _col_tile(n):
    return _pick(n, (512, 256, 128))


def _tile_cols(w):
    d, n = w.shape
    tn = _col_tile(n)
    return w.reshape(d, n // tn, tn).transpose(1, 0, 2)


def _norm_proj(x, g, w_tiles, out_dtype):
    t, d = x.shape
    nn, _, tn = w_tiles.shape
    tm = _pick(t, (1024, 512, 256))
    return pl.pallas_call(
        _norm_proj_kernel,
        grid=(t // tm, nn),
        in_specs=[pl.BlockSpec((tm, d), lambda i, j: (i, 0)),
                  pl.BlockSpec((1, d), lambda i, j: (0, 0)),
                  pl.BlockSpec((None, d, tn), lambda i, j: (j, 0, 0))],
        out_specs=pl.BlockSpec((tm, tn), lambda i, j: (i, j)),
        out_shape=jax.ShapeDtypeStruct((t, nn * tn), out_dtype),
        scratch_shapes=[pltpu.VMEM((tm, d), BF16)],
        compiler_params=_params("parallel", "arbitrary"),
        name="norm_proj",
    )(x, g, w_tiles)


CONV_ROWS = 32


def _conv_branch_kernel(cin_ref, ctx_ref, w_ref, b_ref, lg_ref, lb_ref, c_ref, st_ref, ubuf, sh_ref,
                        *, tm, cc, ktaps, hist):
    s = pl.program_id(1)
    lead = hist - (ktaps - 1)
    sub = STATE_PAD

    @pl.when(s == 0)
    def _():
        ubuf[0:hist, :] = ctx_ref[...]

    cin = cin_ref[...].astype(F32)
    u = cin[:, :cc] * jax.nn.sigmoid(cin[:, cc:])
    ubuf[hist:hist + tm, :] = u
    span = hist + tm - sub
    for r in range(1, sub):
        sh_ref[r - 1, 0:span, :] = ubuf[r:r + span, :]

    bias = b_ref[...]
    lg = lg_ref[...]
    lb = lb_ref[...]
    for r0 in range(0, tm, CONV_ROWS):
        acc = jnp.zeros((CONV_ROWS, cc), F32) + bias
        for k in range(ktaps):
            r = (lead + k) % sub
            a = lead + k - r + r0
            if r == 0:
                tap = ubuf[a:a + CONV_ROWS, :]
            else:
                tap = sh_ref[r - 1, a:a + CONV_ROWS, :]
            acc = acc + w_ref[k:k + 1, :] * tap
        mu = jnp.mean(acc, axis=-1, keepdims=True)
        xc = acc - mu
        var = jnp.mean(xc * xc, axis=-1, keepdims=True)
        y = xc * lax.rsqrt(var + EPS) * lg + lb
        c_ref[r0:r0 + CONV_ROWS, :] = (y * jax.nn.sigmoid(y)).astype(c_ref.dtype)

    tail = ubuf[tm:tm + hist, :]
    st_ref[...] = tail
    ubuf[0:hist, :] = tail


def _conv_branch(slab, col_blk, ctx, dw_w, dw_b, ln_g, ln_b, bsz, seq):
    ktaps, cc = dw_w.shape
    hist = 32
    assert ktaps - 1 <= hist and ctx.shape == (bsz, hist, cc)
    tm = _pick(seq, (512, 256, 128, 64, 32))
    nt = seq // tm
    kern = functools.partial(_conv_branch_kernel, tm=tm, cc=cc, ktaps=ktaps, hist=hist)
    vec = pl.BlockSpec((1, cc), lambda b, s: (0, 0))
    return pl.pallas_call(
        kern,
        grid=(bsz, nt),
        in_specs=[pl.BlockSpec((tm, 2 * cc), lambda b, s: (b * nt + s, col_blk)),
                  pl.BlockSpec((None, hist, cc), lambda b, s: (b, 0, 0)),
                  pl.BlockSpec((ktaps, cc), lambda b, s: (0, 0)),
                  vec, vec, vec],
        out_specs=[pl.BlockSpec((tm, cc), lambda b, s: (b * nt + s, 0)),
                   pl.BlockSpec((None, hist, cc), lambda b, s: (b, 0, 0))],
        out_shape=[jax.ShapeDtypeStruct((bsz * seq, cc), BF16),
                   jax.ShapeDtypeStruct((bsz, hist, cc), F32)],
        scratch_shapes=[pltpu.VMEM((hist + tm, cc), F32),
                        pltpu.VMEM((STATE_PAD - 1, hist + tm - STATE_PAD, cc), F32)],
        compiler_params=_params("arbitrary", "arbitrary"),
        name="conv_branch",
    )(slab, ctx, dw_w, dw_b, ln_g, ln_b)


def _full_attn_kernel(q_ref, kp_ref, vp_ref, kn_ref, vn_ref, lq1_ref, lk1_ref, lq2_ref, lk2_ref,
                      g_ref, o_ref, qq_ref, *, bq, l_init):
    lane = lax.broadcasted_iota(jnp.int32, (bq, HEAD_W), 1)
    q = q_ref[...] * (DIFF_DH ** -0.5)
    zero = jnp.zeros_like(q)
    qq_ref[0:bq, :] = jnp.where(lane < DIFF_DH, q, zero)
    qq_ref[bq:2 * bq, :] = jnp.where(lane >= DIFF_DH, q, zero)
    nt_dims = (((1,), (1,)), ((), ()))
    qq = qq_ref[...]
    s_p = lax.dot_general(qq, kp_ref[...].astype(BF16), nt_dims, preferred_element_type=F32)
    s_n = lax.dot_general(qq, kn_ref[...], nt_dims, preferred_element_type=F32)
    m = jnp.maximum(jnp.max(s_p, axis=-1, keepdims=True), jnp.max(s_n, axis=-1, keepdims=True))
    p_p = jnp.exp(s_p - m)
    p_n = jnp.exp(s_n - m)
    l = jnp.sum(p_p, axis=-1, keepdims=True) + jnp.sum(p_n, axis=-1, keepdims=True)
    o_all = (jnp.dot(p_p.astype(BF16), vp_ref[...].astype(BF16), preferred_element_type=F32)
             + jnp.dot(p_n.astype(BF16), vn_ref[...], preferred_element_type=F32)) / l
    lam = (jnp.exp(jnp.sum(lq1_ref[...] * lk1_ref[...], axis=-1, keepdims=True))
           - jnp.exp(jnp.sum(lq2_ref[...] * lk2_ref[...], axis=-1, keepdims=True)) + l_init)
    o = o_all[0:bq, :] - lam * o_all[bq:2 * bq, :]
    o_ref[...] = (_rms_rows(o, g_ref[...]) * (1.0 - l_init)).astype(o_ref.dtype)


def _causal_attn_kernel(q_ref, k_ref, vt_ref, lq1_ref, lk1_ref, lq2_ref, lk2_ref, g_ref, o_ref,
                        qq_ref, sa_ref, sb_ref, ma_ref, mb_ref, m_ref, acc_ref, *, blk, l_init):
    i = pl.program_id(2)
    lane = lax.broadcasted_iota(jnp.int32, (blk, HEAD_W), 1)
    q = q_ref[...] * (DIFF_DH ** -0.5)
    zero = jnp.zeros_like(q)
    qq_ref[0:blk, :] = jnp.where(lane < DIFF_DH, q, zero)
    qq_ref[blk:2 * blk, :] = jnp.where(lane >= DIFF_DH, q, zero)
    m_ref[...] = jnp.full(m_ref.shape, -jnp.inf, F32)
    acc_ref[...] = jnp.zeros(acc_ref.shape, F32)

    def scores(j, s_ref, mx_ref):
        start = pl.multiple_of(j * blk, blk)
        st = lax.dot_general(k_ref[pl.ds(start, blk), :], qq_ref[...], (((1,), (1,)), ((), ())),
                             preferred_element_type=F32)
        s_ref[...] = st
        mx_ref[...] = jnp.max(st, axis=0, keepdims=True)

    def update(j, s_ref, mx_ref, masked):
        st = s_ref[...]
        if masked:
            kchunk = lax.broadcasted_iota(jnp.int32, (blk, 1), 0) // CHUNK
            qcol = lax.broadcasted_iota(jnp.int32, (1, 2 * blk), 1)
            qchunk = jnp.where(qcol >= blk, qcol - blk, qcol) // CHUNK
            st = jnp.where(kchunk <= qchunk, st, -jnp.inf)
            mx = jnp.max(st, axis=0, keepdims=True)
        else:
            mx = mx_ref[...]
        m_prev = m_ref[...]
        m_new = jnp.maximum(m_prev, mx)
        alpha = jnp.exp(m_prev - m_new)
        pt = jnp.exp(st - m_new).astype(BF16)
        acc_ref[...] = alpha * acc_ref[...] + jnp.dot(vt_ref[j], pt, preferred_element_type=F32)
        m_ref[...] = m_new

    scores(0, sa_ref, ma_ref)

    def pair(jp, carry):
        j = 2 * jp
        scores(j + 1, sb_ref, mb_ref)
        update(j, sa_ref, ma_ref, False)
        scores(j + 2, sa_ref, ma_ref)
        update(j + 1, sb_ref, mb_ref, False)
        return carry

    lax.fori_loop(0, i // 2, pair, 0)

    @pl.when(i % 2 == 1)
    def _():
        scores(i, sb_ref, mb_ref)
        update(i - 1, sa_ref, ma_ref, False)
        update(i, sb_ref, mb_ref, True)

    @pl.when(i % 2 == 0)
    def _():
        update(i, sa_ref, ma_ref, True)

    lam = (jnp.exp(jnp.sum(lq1_ref[...] * lk1_ref[...], axis=-1, keepdims=True))
           - jnp.exp(jnp.sum(lq2_ref[...] * lk2_ref[...], axis=-1, keepdims=True)) + l_init)
    acc = acc_ref[...]
    o_all = acc[0:HEAD_W, :] / acc[HEAD_W:HEAD_W + 1, :]
    ot = o_all[:, 0:blk] - lam * o_all[:, blk:2 * blk]
    ot = ot * lax.rsqrt(jnp.mean(ot * ot, axis=0, keepdims=True) + EPS) * (1.0 - l_init)
    o_ref[...] = (ot.T * g_ref[...]).astype(o_ref.dtype)


def _causal_attn(slab, q_col0, kb, vt, lam_p, subln_g, bsz, seq, l_init):
    blk = ATT_BLK
    assert seq % blk == 0 and blk % CHUNK == 0
    nq = seq // blk
    kern = functools.partial(_causal_attn_kernel, blk=blk, l_init=l_init)
    lam_blk = pl.BlockSpec((1, DIFF_DH), lambda b, h, i: (0, 0))
    return pl.pallas_call(
        kern,
        grid=(bsz, DIFF_H, nq),
        in_specs=[pl.BlockSpec((blk, HEAD_W), lambda b, h, i: (b * nq + i, q_col0 + h)),
                  pl.BlockSpec((None, seq, HEAD_W), lambda b, h, i: (b, 0, h)),
                  pl.BlockSpec((None, None, nq, VT_ROWS, blk), lambda b, h, i: (b, h, 0, 0, 0)),
                  lam_blk, lam_blk, lam_blk, lam_blk,
                  pl.BlockSpec((1, HEAD_W), lambda b, h, i: (0, 0))],
        out_specs=pl.BlockSpec((blk, HEAD_W), lambda b, h, i: (b * nq + i, h)),
        out_shape=jax.ShapeDtypeStruct((bsz * seq, DIFF_W), BF16),
        scratch_shapes=[pltpu.VMEM((2 * blk, HEAD_W), BF16),
                        pltpu.VMEM((blk, 2 * blk), F32), pltpu.VMEM((blk, 2 * blk), F32),
                        pltpu.VMEM((1, 2 * blk), F32), pltpu.VMEM((1, 2 * blk), F32),
                        pltpu.VMEM((1, 2 * blk), F32),
                        pltpu.VMEM((VT_ROWS, 2 * blk), F32)],
        compiler_params=_params("parallel", "parallel", "arbitrary"),
        name="causal_attn",
    )(slab, kb, vt, *lam_p, subln_g)


def _full_attn(slab, q_col0, k_past, v_past, layer_idx, kb, vb, lam_p, subln_g, bsz, sq, l_init):
    past = k_past.shape[2]
    kern = functools.partial(_full_attn_kernel, bq=sq, l_init=l_init)
    lam_blk = pl.BlockSpec((1, DIFF_DH), lambda b, h: (0, 0))
    past_blk = pl.BlockSpec((None, None, past, HEAD_W), lambda b, h: (layer_idx, b, 0, h))
    new_blk = pl.BlockSpec((sq, HEAD_W), lambda b, h: (b, h))
    return pl.pallas_call(
        kern,
        grid=(bsz, DIFF_H),
        in_specs=[pl.BlockSpec((sq, HEAD_W), lambda b, h: (b, q_col0 + h)),
                  past_blk, past_blk, new_blk, new_blk, lam_blk, lam_blk, lam_blk, lam_blk,
                  pl.BlockSpec((1, HEAD_W), lambda b, h: (0, 0))],
        out_specs=pl.BlockSpec((sq, HEAD_W), lambda b, h: (b, h)),
        out_shape=jax.ShapeDtypeStruct((bsz * sq, DIFF_W), BF16),
        scratch_shapes=[pltpu.VMEM((2 * sq, HEAD_W), BF16)],
        compiler_params=_params("parallel", "parallel"),
        name="full_attn",
    )(slab, k_past, v_past, kb, vb, *lam_p, subln_g)


def _merge_kernel(x_ref, c_ref, on_ref, mq_ref, gate_ref, mk_ref, mv_ref,
                  wco_ref, wdo_ref, wmo_ref, wout_ref, o_ref, om_ref, *, d):
    scale = MEM_DH ** -0.5
    for h in range(MEM_H):
        cols = slice(h * MEM_DH, (h + 1) * MEM_DH)
        s = lax.dot_general(mq_ref[:, cols], mk_ref[:, cols], (((1,), (1,)), ((), ())),
                            preferred_element_type=F32) * scale
        p = jnp.exp(s - jnp.max(s, axis=-1, keepdims=True))
        p = p / jnp.sum(p, axis=-1, keepdims=True)
        om_ref[:, cols] = jnp.dot(p.astype(BF16), mv_ref[:, cols],
                                  preferred_element_type=F32).astype(BF16)

    def gate(b):
        return jax.nn.sigmoid(gate_ref[:, b * d:(b + 1) * d].astype(F32))

    mix = gate(0) * jnp.dot(c_ref[...], wco_ref[...], preferred_element_type=F32)
    mix = mix + gate(1) * jnp.dot(on_ref[...], wdo_ref[...], preferred_element_type=F32)
    mix = mix + gate(2) * jnp.dot(om_ref[...], wmo_ref[...], preferred_element_type=F32)
    o_ref[...] = x_ref[...] + jnp.dot(mix.astype(BF16), wout_ref[...], preferred_element_type=F32)


def _merge(x, c, on, slab, mq_blk, mk, mv, wco, wdo, wmo, wout, bsz, seq):
    t, d = x.shape
    cc = c.shape[1]
    mlen = mk.shape[1]
    tm = _pick(seq, (256, 128, 64, 32))
    nt = seq // tm
    row = lambda i: (i, 0)
    full = lambda a: pl.BlockSpec(a.shape, lambda i: (0, 0))
    mem_blk = pl.BlockSpec((None, mlen, MEM_W), lambda i: (i // nt, 0, 0))
    return pl.pallas_call(
        functools.partial(_merge_kernel, d=d),
        grid=(t // tm,),
        in_specs=[pl.BlockSpec((tm, d), row), pl.BlockSpec((tm, cc), row),
                  pl.BlockSpec((tm, DIFF_W), row),
                  pl.BlockSpec((tm, MEM_W), lambda i: (i, mq_blk)),
                  pl.BlockSpec((tm, N_BRANCH * d), row),
                  mem_blk, mem_blk, full(wco), full(wdo), full(wmo), full(wout)],
        out_specs=pl.BlockSpec((tm, d), row),
        out_shape=jax.ShapeDtypeStruct((t, d), F32),
        scratch_shapes=[pltpu.VMEM((tm, MEM_W), BF16)],
        compiler_params=_params("parallel"),
        name="merge",
    )(x, c, on, slab, slab, mk, mv, wco, wdo, wmo, wout)


def _ffn_body(x_ref, g_ref, wg_ref, wu_ref, wd_ref, dw_ref, db_ref, ctx_ref, fin_ref,
              o_ref, st_ref, h_ref, acc_ref, ext_ref, tail_ref, acta_ref, actb_ref,
              *, tm, tf, nt, nf, final_norm):
    i = pl.program_id(0)
    j = pl.program_id(1)
    s = i % nt

    def gate_up(act_ref):
        h = h_ref[...]
        fg = jnp.dot(h, wg_ref[...], preferred_element_type=F32)
        fu = jnp.dot(h, wu_ref[...], preferred_element_type=F32)
        ext_ref[0:STATE_PAD, :] = jnp.where(s == 0, ctx_ref[...], tail_ref[j])
        ext_ref[STATE_PAD:STATE_PAD + tm, :] = fg
        last = fg[tm - STATE_PAD:tm, :]
        tail_ref[j] = last
        st_ref[:, pl.ds(pl.multiple_of(j * tf, tf), tf)] = last
        y = (dw_ref[0:1, :] * ext_ref[STATE_PAD - 2:STATE_PAD - 2 + tm, :]
             + dw_ref[1:2, :] * ext_ref[STATE_PAD - 1:STATE_PAD - 1 + tm, :]
             + dw_ref[2:3, :] * fg + db_ref[...])
        act_ref[...] = (y * jax.nn.sigmoid(y) * fu).astype(BF16)

    def down(act_ref):
        acc_ref[...] += jnp.dot(act_ref[...], wd_ref[...], preferred_element_type=F32)

    slots = (acta_ref, actb_ref)

    @pl.when(j == 0)
    def _():
        @pl.when(i == 0)
        def _():
            tail_ref[...] = jnp.zeros(tail_ref.shape, F32)

        h_ref[...] = _rms_rows(x_ref[...], g_ref[...]).astype(BF16)
        acc_ref[...] = jnp.zeros(acc_ref.shape, F32)
        gate_up(slots[0])

    for parity in (0, 1):
        @pl.when((j > 0) & (j < nf) & (j % 2 == parity))
        def _():
            down(slots[1 - parity])
            gate_up(slots[parity])

    @pl.when(j == nf)
    def _():
        down(slots[(nf - 1) % 2])
        out = x_ref[...] + acc_ref[...]
        if final_norm:
            out = _rms_rows(out, fin_ref[...])
        o_ref[...] = out


def _ffn(x, g, wgu, wd, dw_w, dw_b, ctx, fin_g, bsz, seq, final_norm):
    t, d = x.shape
    ff = wd.shape[0]
    tf = wgu.shape[2]
    assert dw_w.shape[0] == 3 and ctx.shape == (bsz, STATE_PAD, ff)
    tm = _pick(seq, (512, 256, 128, 64, 32))
    nt, nf = seq // tm, ff // tf
    assert nf >= 2 and wgu.shape[0] == 2 * nf
    kern = functools.partial(_ffn_body, tm=tm, tf=tf, nt=nt, nf=nf, final_norm=final_norm)
    cur = lambda j: jnp.minimum(j, nf - 1)
    prev = lambda j: jnp.maximum(j - 1, 0)
    return pl.pallas_call(
        kern,
        grid=(t // tm, nf + 1),
        in_specs=[pl.BlockSpec((tm, d), lambda i, j: (i, 0)),
                  pl.BlockSpec((1, d), lambda i, j: (0, 0)),
                  pl.BlockSpec((None, d, tf), lambda i, j: (cur(j), 0, 0)),
                  pl.BlockSpec((None, d, tf), lambda i, j: (nf + cur(j), 0, 0)),
                  pl.BlockSpec((tf, d), lambda i, j: (prev(j), 0)),
                  pl.BlockSpec((3, tf), lambda i, j: (0, cur(j))),
                  pl.BlockSpec((1, tf), lambda i, j: (0, cur(j))),
                  pl.BlockSpec((None, STATE_PAD, tf), lambda i, j: (i // nt, 0, cur(j))),
                  pl.BlockSpec((1, d), lambda i, j: (0, 0))],
        out_specs=[pl.BlockSpec((tm, d), lambda i, j: (i, 0)),
                   pl.BlockSpec((None, STATE_PAD, ff), lambda i, j: (i // nt, 0, 0))],
        out_shape=[jax.ShapeDtypeStruct((t, d), F32),
                   jax.ShapeDtypeStruct((bsz, STATE_PAD, ff), F32)],
        scratch_shapes=[pltpu.VMEM((tm, d), BF16), pltpu.VMEM((tm, d), F32),
                        pltpu.VMEM((STATE_PAD + tm, tf), F32),
                        pltpu.VMEM((nf, STATE_PAD, tf), F32),
                        pltpu.VMEM((tm, tf), BF16), pltpu.VMEM((tm, tf), BF16)],
        compiler_params=_params("arbitrary", "arbitrary"),
        name="conv_ffn",
    )(x, g, wgu, wgu, wd, dw_w, dw_b, ctx, fin_g)


def _row(v):
    return v.reshape(1, -1)


def _pad_state(st, rows):
    return jnp.pad(st, ((0, 0), (rows - st.shape[1], 0), (0, 0)))


def _layer(x, bsz, seq, layer_idx, conv_ctx, ffn_ctx, k_past, v_past, mk, mv, P, final_g, final_norm):
    t, d = x.shape
    cc = P["conv_dw_w"].shape[1]
    l_init = _lambda_init(layer_idx)

    slab = _norm_proj(x, P["norm1_g"], P["w_main"], BF16)
    gate_w = N_BRANCH * d
    conv_blk = gate_w // (2 * cc)
    q_col0 = (gate_w + 2 * cc) // HEAD_W
    mq_blk = (gate_w + 2 * cc + DIFF_W) // MEM_W
    assert gate_w % (2 * cc) == 0 and (gate_w + 2 * cc + DIFF_W) % MEM_W == 0

    c, conv_state = _conv_branch(slab, conv_blk, conv_ctx, P["conv_dw_w"], P["conv_dw_b"],
                                 P["conv_ln_g"], P["conv_ln_b"], bsz, seq)

    if k_past is None:
        k, v, kb, vt = _kv_proj_vt(x, P["norm1_g"], P["w_kv"], bsz, seq)
        on = _causal_attn(slab, q_col0, kb.reshape(bsz, seq, DIFF_W), vt, P["lam"],
                          P["diff_subln_g"], bsz, seq, l_init)
    else:
        k, v, kb, vb = _kv_proj(x, P["norm1_g"], P["w_kv"])
        on = _full_attn(slab, q_col0, k_past, v_past, layer_idx, kb, vb, P["lam"],
                        P["diff_subln_g"], bsz, seq, l_init)

    x1 = _merge(x, c, on, slab, mq_blk, mk, mv, P["w_conv_out"], P["w_diff_out"],
                P["w_mem_out"], P["w_out"], bsz, seq)
    x2, ffn_state = _ffn(x1, P["norm2_g"], P["w_ffn_gu"], P["w_ffn_down"], P["ffn_dw_w"],
                         P["ffn_dw_b"], ffn_ctx, final_g, bsz, seq, final_norm)
    return x2, k, v, conv_state, ffn_state


def kernel(x_prompt, x_sample, mem_prompt, cache_k, cache_v, cache_mem_k, cache_mem_v, state_conv, state_ffn_conv, norm1_g, w_in, conv_dw_w, conv_dw_b, conv_ln_g, conv_ln_b, w_conv_out, lam_q1, lam_k1, lam_q2, lam_k2, diff_subln_g, w_diff_out, mem_norm_g, w_mem_kv, w_mem_out, w_out, norm2_g, w_ffn_gu, ffn_dw_w, ffn_dw_b, w_ffn_down, final_g):
    depth = w_in.shape[0]
    bp, sp, d = x_prompt.shape
    bs, ss, _ = x_sample.shape
    mlen = mem_prompt.shape[1]
    ktaps, cc = conv_dw_w.shape[1:]
    ff = w_ffn_down.shape[1]
    past = cache_k.shape[2]
    conv_hist = 32

    xp = x_prompt.reshape(bp * sp, d)
    xs = x_sample.reshape(bs * ss, d)
    mem = mem_prompt.reshape(bp * mlen, d)
    fin = _row(final_g)

    c0, c1, c2, c3, c4 = (2 * cc, 2 * cc + DIFF_W, 2 * cc + 2 * DIFF_W, 2 * cc + 3 * DIFF_W,
                          2 * cc + 3 * DIFF_W + MEM_W)
    outs = {n: [] for n in ("kp", "vp", "mkp", "mvp", "cp", "fp", "ks", "vs", "cs", "fs")}
    for l in range(depth):
        wl = w_in[l].astype(BF16)
        P = {
            "norm1_g": _row(norm1_g[l]),
            "w_kv": wl[:, c1:c3],
            "w_main": _tile_cols(jnp.concatenate(
                [wl[:, c4:], wl[:, :c0], wl[:, c0:c1], wl[:, c3:c4]], axis=1)),
            "conv_dw_w": conv_dw_w[l], "conv_dw_b": _row(conv_dw_b[l]),
            "conv_ln_g": _row(conv_ln_g[l]), "conv_ln_b": _row(conv_ln_b[l]),
            "w_conv_out": w_conv_out[l].astype(BF16),
            "lam": (_row(lam_q1[l]), _row(lam_k1[l]), _row(lam_q2[l]), _row(lam_k2[l])),
            "diff_subln_g": _row(diff_subln_g[l]),
            "w_diff_out": w_diff_out[l].astype(BF16),
            "w_mem_out": w_mem_out[l].astype(BF16),
            "w_out": w_out[l].astype(BF16),
            "norm2_g": _row(norm2_g[l]),
            "w_ffn_gu": _tile_cols(w_ffn_gu[l].astype(BF16)),
            "ffn_dw_w": ffn_dw_w[l], "ffn_dw_b": _row(ffn_dw_b[l]),
            "w_ffn_down": w_ffn_down[l].astype(BF16),
        }
        final_norm = l == depth - 1

        mkv = _norm_proj(mem, _row(mem_norm_g[l]), _tile_cols(w_mem_kv[l].astype(BF16)), F32)
        mk_p = mkv[:, :MEM_W].reshape(bp, mlen, MEM_W)
        mv_p = mkv[:, MEM_W:].reshape(bp, mlen, MEM_W)
        xp, kp, vp, cp, fp = _layer(
            xp, bp, sp, l, jnp.zeros((bp, conv_hist, cc), F32), jnp.zeros((bp, STATE_PAD, ff), F32),
            None, None, mk_p.astype(BF16), mv_p.astype(BF16), P, fin, final_norm)
        outs["kp"].append(kp.reshape(bp, sp, DIFF_H, HEAD_W))
        outs["vp"].append(vp.reshape(bp, sp, DIFF_H, HEAD_W))
        outs["mkp"].append(mk_p.reshape(bp, mlen, MEM_H, MEM_DH))
        outs["mvp"].append(mv_p.reshape(bp, mlen, MEM_H, MEM_DH))
        outs["cp"].append(cp[:, conv_hist - (ktaps - 1):])
        outs["fp"].append(fp[:, STATE_PAD - 2:])

        xs, ks_, vs_, cs_, fs_ = _layer(
            xs, bs, ss, l, _pad_state(state_conv[l], conv_hist), _pad_state(state_ffn_conv[l], STATE_PAD),
            cache_k.reshape(depth, bs, past, DIFF_W), cache_v.reshape(depth, bs, past, DIFF_W),
            cache_mem_k[l].reshape(bs, mlen, MEM_W).astype(BF16),
            cache_mem_v[l].reshape(bs, mlen, MEM_W).astype(BF16), P, fin, final_norm)
        outs["ks"].append(ks_.reshape(bs, ss, DIFF_H, HEAD_W))
        outs["vs"].append(vs_.reshape(bs, ss, DIFF_H, HEAD_W))
        outs["cs"].append(cs_[:, conv_hist - (ktaps - 1):])
        outs["fs"].append(fs_[:, STATE_PAD - 2:])

    st = lambda n: jnp.stack(outs[n])
    return (xp.reshape(bp, sp, d), xs.reshape(bs, ss, d),
            st("kp"), st("vp"), st("mkp"), st("mvp"), st("cp"), st("fp"),
            st("ks"), st("vs"), st("cs"), st("fs"))
```

```python
import functools
import math

import jax
import jax.numpy as jnp
from jax import lax
from jax.experimental import pallas as pl
from jax.experimental.pallas import tpu as pltpu

F32 = jnp.float32
BF16 = jnp.bfloat16

EPS = 1e-6
CHUNK = 64
DIFF_H = 8
DIFF_DH = 64
HEAD_W = 2 * DIFF_DH
DIFF_W = DIFF_H * HEAD_W
MEM_H = 4
MEM_DH = 128
MEM_W = MEM_H * MEM_DH
N_BRANCH = 3
STATE_PAD = 8

VMEM_LIMIT_BYTES = 56 * 1024 * 1024


def _lambda_init(layer_idx):
    return 0.8 - 0.6 * math.exp(-0.3 * layer_idx)


def _params(*sem):
    return pltpu.CompilerParams(dimension_semantics=sem, vmem_limit_bytes=VMEM_LIMIT_BYTES)


def _pick(n, prefs):
    for p in prefs:
        if n % p == 0:
            return p
    return n


def _rms_rows(x, g):
    return x * lax.rsqrt(jnp.mean(x * x, axis=-1, keepdims=True) + EPS) * g


def _kv_proj_kernel(x_ref, g_ref, w_ref, k_ref, v_ref, kb_ref, vb_ref):
    h = _rms_rows(x_ref[...], g_ref[...]).astype(BF16)
    kv = jnp.dot(h, w_ref[...], preferred_element_type=F32)
    k = kv[:, :DIFF_W]
    v = kv[:, DIFF_W:]
    k_ref[...] = k
    v_ref[...] = v
    kb_ref[...] = k.astype(BF16)
    vb_ref[...] = v.astype(BF16)


def _kv_proj(x, g, w_kv):
    t, d = x.shape
    tm = _pick(t, (512, 256))
    row = lambda i: (i, 0)
    const = lambda i: (0, 0)
    out_blk = pl.BlockSpec((tm, DIFF_W), row)
    return pl.pallas_call(
        _kv_proj_kernel,
        grid=(t // tm,),
        in_specs=[pl.BlockSpec((tm, d), row), pl.BlockSpec((1, d), const),
                  pl.BlockSpec((d, 2 * DIFF_W), const)],
        out_specs=[out_blk, out_blk, out_blk, out_blk],
        out_shape=[jax.ShapeDtypeStruct((t, DIFF_W), F32), jax.ShapeDtypeStruct((t, DIFF_W), F32),
                   jax.ShapeDtypeStruct((t, DIFF_W), BF16), jax.ShapeDtypeStruct((t, DIFF_W), BF16)],
        compiler_params=_params("parallel"),
        name="kv_proj",
    )(x, g, w_kv)


ATT_BLK = 512
VT_ROWS = HEAD_W + 16


def _kv_proj_vt_kernel(x_ref, g_ref, w_ref, k_ref, v_ref, kb_ref, vt_ref):
    h = _rms_rows(x_ref[...], g_ref[...]).astype(BF16)
    kv = jnp.dot(h, w_ref[...], preferred_element_type=F32)
    k = kv[:, :DIFF_W]
    v = kv[:, DIFF_W:]
    k_ref[...] = k
    v_ref[...] = v
    kb_ref[...] = k.astype(BF16)
    tm = v.shape[0]
    for hd in range(DIFF_H):
        vt_ref[hd, 0:HEAD_W, :] = v[:, hd * HEAD_W:(hd + 1) * HEAD_W].T.astype(BF16)
        vt_ref[hd, HEAD_W:VT_ROWS, :] = jnp.ones((VT_ROWS - HEAD_W, tm), BF16)


def _kv_proj_vt(x, g, w_kv, bsz, seq):
    t, d = x.shape
    tm = ATT_BLK
    nb = seq // tm
    row = lambda i: (i, 0)
    const = lambda i: (0, 0)
    out_blk = pl.BlockSpec((tm, DIFF_W), row)
    return pl.pallas_call(
        _kv_proj_vt_kernel,
        grid=(t // tm,),
        in_specs=[pl.BlockSpec((tm, d), row), pl.BlockSpec((1, d), const),
                  pl.BlockSpec((d, 2 * DIFF_W), const)],
        out_specs=[out_blk, out_blk, out_blk,
                   pl.BlockSpec((None, DIFF_H, None, VT_ROWS, tm),
                                lambda i: (i // nb, 0, i % nb, 0, 0))],
        out_shape=[jax.ShapeDtypeStruct((t, DIFF_W), F32), jax.ShapeDtypeStruct((t, DIFF_W), F32),
                   jax.ShapeDtypeStruct((t, DIFF_W), BF16),
                   jax.ShapeDtypeStruct((bsz, DIFF_H, nb, VT_ROWS, tm), BF16)],
        compiler_params=_params("parallel"),
        name="kv_proj_vt",
    )(x, g, w_kv)


def _norm_proj_kernel(x_ref, g_ref, w_ref, o_ref, h_ref):
    @pl.when(pl.program_id(1) == 0)
    def _():
        h_ref[...] = _rms_rows(x_ref[...], g_ref[...]).astype(BF16)

    o_ref[...] = jnp.dot(h_ref[...], w_ref[...], preferred_element_type=F32).astype(o_ref.dtype)


def _norm_proj(x, g, w, out_dtype):
    t, d = x.shape
    n = w.shape[1]
    tm = _pick(t, (1024, 512, 256))
    tn = _pick(n, (512, 256, 128))
    return pl.pallas_call(
        _norm_proj_kernel,
        grid=(t // tm, n // tn),
        in_specs=[pl.BlockSpec((tm, d), lambda i, j: (i, 0)),
                  pl.BlockSpec((1, d), lambda i, j: (0, 0)),
                  pl.BlockSpec((d, tn), lambda i, j: (0, j))],
        out_specs=pl.BlockSpec((tm, tn), lambda i, j: (i, j)),
        out_shape=jax.ShapeDtypeStruct((t, n), out_dtype),
        scratch_shapes=[pltpu.VMEM((tm, d), BF16)],
        compiler_params=_params("parallel", "arbitrary"),
        name="norm_proj",
    )(x, g, w)


CONV_ROWS = 32


def _conv_branch_kernel(cin_ref, ctx_ref, w_ref, b_ref, lg_ref, lb_ref, c_ref, st_ref, ubuf, sh_ref,
                        *, tm, cc, ktaps, hist):
    s = pl.program_id(1)
    lead = hist - (ktaps - 1)
    sub = STATE_PAD

    @pl.when(s == 0)
    def _():
        ubuf[0:hist, :] = ctx_ref[...]

    cin = cin_ref[...].astype(F32)
    u = cin[:, :cc] * jax.nn.sigmoid(cin[:, cc:])
    ubuf[hist:hist + tm, :] = u
    span = hist + tm - sub
    for r in range(1, sub):
        sh_ref[r - 1, 0:span, :] = ubuf[r:r + span, :]

    bias = b_ref[...]
    lg = lg_ref[...]
    lb = lb_ref[...]
    for r0 in range(0, tm, CONV_ROWS):
        acc = jnp.zeros((CONV_ROWS, cc), F32) + bias
        for k in range(ktaps):
            r = (lead + k) % sub
            a = lead + k - r + r0
            if r == 0:
                tap = ubuf[a:a + CONV_ROWS, :]
            else:
                tap = sh_ref[r - 1, a:a + CONV_ROWS, :]
            acc = acc + w_ref[k:k + 1, :] * tap
        mu = jnp.mean(acc, axis=-1, keepdims=True)
        xc = acc - mu
        var = jnp.mean(xc * xc, axis=-1, keepdims=True)
        y = xc * lax.rsqrt(var + EPS) * lg + lb
        c_ref[r0:r0 + CONV_ROWS, :] = (y * jax.nn.sigmoid(y)).astype(c_ref.dtype)

    tail = ubuf[tm:tm + hist, :]
    st_ref[...] = tail
    ubuf[0:hist, :] = tail


def _conv_branch(slab, col_blk, ctx, dw_w, dw_b, ln_g, ln_b, bsz, seq):
    ktaps, cc = dw_w.shape
    hist = 32
    assert ktaps - 1 <= hist and ctx.shape == (bsz, hist, cc)
    tm = _pick(seq, (512, 256, 128, 64, 32))
    nt = seq // tm
    kern = functools.partial(_conv_branch_kernel, tm=tm, cc=cc, ktaps=ktaps, hist=hist)
    vec = pl.BlockSpec((1, cc), lambda b, s: (0, 0))
    return pl.pallas_call(
        kern,
        grid=(bsz, nt),
        in_specs=[pl.BlockSpec((tm, 2 * cc), lambda b, s: (b * nt + s, col_blk)),
                  pl.BlockSpec((None, hist, cc), lambda b, s: (b, 0, 0)),
                  pl.BlockSpec((ktaps, cc), lambda b, s: (0, 0)),
                  vec, vec, vec],
        out_specs=[pl.BlockSpec((tm, cc), lambda b, s: (b * nt + s, 0)),
                   pl.BlockSpec((None, hist, cc), lambda b, s: (b, 0, 0))],
        out_shape=[jax.ShapeDtypeStruct((bsz * seq, cc), BF16),
                   jax.ShapeDtypeStruct((bsz, hist, cc), F32)],
        scratch_shapes=[pltpu.VMEM((hist + tm, cc), F32),
                        pltpu.VMEM((STATE_PAD - 1, hist + tm - STATE_PAD, cc), F32)],
        compiler_params=_params("arbitrary", "arbitrary"),
        name="conv_branch",
    )(slab, ctx, dw_w, dw_b, ln_g, ln_b)


def _full_attn_kernel(q_ref, kp_ref, vp_ref, kn_ref, vn_ref, lq1_ref, lk1_ref, lq2_ref, lk2_ref,
                      g_ref, o_ref, qq_ref, *, bq, l_init):
    lane = lax.broadcasted_iota(jnp.int32, (bq, HEAD_W), 1)
    q = q_ref[...] * (DIFF_DH ** -0.5)
    zero = jnp.zeros_like(q)
    qq_ref[0:bq, :] = jnp.where(lane < DIFF_DH, q, zero)
    qq_ref[bq:2 * bq, :] = jnp.where(lane >= DIFF_DH, q, zero)
    nt_dims = (((1,), (1,)), ((), ()))
    qq = qq_ref[...]
    s_p = lax.dot_general(qq, kp_ref[...].astype(BF16), nt_dims, preferred_element_type=F32)
    s_n = lax.dot_general(qq, kn_ref[...], nt_dims, preferred_element_type=F32)
    m = jnp.maximum(jnp.max(s_p, axis=-1, keepdims=True), jnp.max(s_n, axis=-1, keepdims=True))
    p_p = jnp.exp(s_p - m)
    p_n = jnp.exp(s_n - m)
    l = jnp.sum(p_p, axis=-1, keepdims=True) + jnp.sum(p_n, axis=-1, keepdims=True)
    o_all = (jnp.dot(p_p.astype(BF16), vp_ref[...].astype(BF16), preferred_element_type=F32)
             + jnp.dot(p_n.astype(BF16), vn_ref[...], preferred_element_type=F32)) / l
    lam = (jnp.exp(jnp.sum(lq1_ref[...] * lk1_ref[...], axis=-1, keepdims=True))
           - jnp.exp(jnp.sum(lq2_ref[...] * lk2_ref[...], axis=-1, keepdims=True)) + l_init)
    o = o_all[0:bq, :] - lam * o_all[bq:2 * bq, :]
    o_ref[...] = (_rms_rows(o, g_ref[...]) * (1.0 - l_init)).astype(o_ref.dtype)


def _causal_attn_kernel(q_ref, k_ref, vt_ref, lq1_ref, lk1_ref, lq2_ref, lk2_ref, g_ref, o_ref,
                        qq_ref, sa_ref, sb_ref, ma_ref, mb_ref, m_ref, acc_ref, *, blk, l_init):
    i = pl.program_id(2)
    lane = lax.broadcasted_iota(jnp.int32, (blk, HEAD_W), 1)
    q = q_ref[...] * (DIFF_DH ** -0.5)
    zero = jnp.zeros_like(q)
    qq_ref[0:blk, :] = jnp.where(lane < DIFF_DH, q, zero)
    qq_ref[blk:2 * blk, :] = jnp.where(lane >= DIFF_DH, q, zero)
    m_ref[...] = jnp.full(m_ref.shape, -jnp.inf, F32)
    acc_ref[...] = jnp.zeros(acc_ref.shape, F32)

    def scores(j, s_ref, mx_ref):
        start = pl.multiple_of(j * blk, blk)
        st = lax.dot_general(k_ref[pl.ds(start, blk), :], qq_ref[...], (((1,), (1,)), ((), ())),
                             preferred_element_type=F32)
        s_ref[...] = st
        mx_ref[...] = jnp.max(st, axis=0, keepdims=True)

    def update(j, s_ref, mx_ref, masked):
        st = s_ref[...]
        if masked:
            kchunk = lax.broadcasted_iota(jnp.int32, (blk, 1), 0) // CHUNK
            qcol = lax.broadcasted_iota(jnp.int32, (1, 2 * blk), 1)
            qchunk = jnp.where(qcol >= blk, qcol - blk, qcol) // CHUNK
            st = jnp.where(kchunk <= qchunk, st, -jnp.inf)
            mx = jnp.max(st, axis=0, keepdims=True)
        else:
            mx = mx_ref[...]
        m_prev = m_ref[...]
        m_new = jnp.maximum(m_prev, mx)
        alpha = jnp.exp(m_prev - m_new)
        pt = jnp.exp(st - m_new).astype(BF16)
        acc_ref[...] = alpha * acc_ref[...] + jnp.dot(vt_ref[j], pt, preferred_element_type=F32)
        m_ref[...] = m_new

    scores(0, sa_ref, ma_ref)

    def pair(jp, carry):
        j = 2 * jp
        scores(j + 1, sb_ref, mb_ref)
        update(j, sa_ref, ma_ref, False)
        scores(j + 2, sa_ref, ma_ref)
        update(j + 1, sb_ref, mb_ref, False)
        return carry

    lax.fori_loop(0, i // 2, pair, 0)

    @pl.when(i % 2 == 1)
    def _():
        scores(i, sb_ref, mb_ref)
        update(i - 1, sa_ref, ma_ref, False)
        update(i, sb_ref, mb_ref, True)

    @pl.when(i % 2 == 0)
    def _():
        update(i, sa_ref, ma_ref, True)

    lam = (jnp.exp(jnp.sum(lq1_ref[...] * lk1_ref[...], axis=-1, keepdims=True))
           - jnp.exp(jnp.sum(lq2_ref[...] * lk2_ref[...], axis=-1, keepdims=True)) + l_init)
    acc = acc_ref[...]
    o_all = acc[0:HEAD_W, :] / acc[HEAD_W:HEAD_W + 1, :]
    ot = o_all[:, 0:blk] - lam * o_all[:, blk:2 * blk]
    ot = ot * lax.rsqrt(jnp.mean(ot * ot, axis=0, keepdims=True) + EPS) * (1.0 - l_init)
    o_ref[...] = (ot.T * g_ref[...]).astype(o_ref.dtype)


def _causal_attn(slab, q_col0, kb, vt, lam_p, subln_g, bsz, seq, l_init):
    blk = ATT_BLK
    assert seq % blk == 0 and blk % CHUNK == 0
    nq = seq // blk
    kern = functools.partial(_causal_attn_kernel, blk=blk, l_init=l_init)
    lam_blk = pl.BlockSpec((1, DIFF_DH), lambda b, h, i: (0, 0))
    return pl.pallas_call(
        kern,
        grid=(bsz, DIFF_H, nq),
        in_specs=[pl.BlockSpec((blk, HEAD_W), lambda b, h, i: (b * nq + i, q_col0 + h)),
                  pl.BlockSpec((None, seq, HEAD_W), lambda b, h, i: (b, 0, h)),
                  pl.BlockSpec((None, None, nq, VT_ROWS, blk), lambda b, h, i: (b, h, 0, 0, 0)),
                  lam_blk, lam_blk, lam_blk, lam_blk,
                  pl.BlockSpec((1, HEAD_W), lambda b, h, i: (0, 0))],
        out_specs=pl.BlockSpec((blk, HEAD_W), lambda b, h, i: (b * nq + i, h)),
        out_shape=jax.ShapeDtypeStruct((bsz * seq, DIFF_W), BF16),
        scratch_shapes=[pltpu.VMEM((2 * blk, HEAD_W), BF16),
                        pltpu.VMEM((blk, 2 * blk), F32), pltpu.VMEM((blk, 2 * blk), F32),
                        pltpu.VMEM((1, 2 * blk), F32), pltpu.VMEM((1, 2 * blk), F32),
                        pltpu.VMEM((1, 2 * blk), F32),
                        pltpu.VMEM((VT_ROWS, 2 * blk), F32)],
        compiler_params=_params("parallel", "parallel", "arbitrary"),
        name="causal_attn",
    )(slab, kb, vt, *lam_p, subln_g)


def _full_attn(slab, q_col0, k_past, v_past, kb, vb, lam_p, subln_g, bsz, sq, l_init):
    past = k_past.shape[1]
    kern = functools.partial(_full_attn_kernel, bq=sq, l_init=l_init)
    lam_blk = pl.BlockSpec((1, DIFF_DH), lambda b, h: (0, 0))
    past_blk = pl.BlockSpec((None, past, HEAD_W), lambda b, h: (b, 0, h))
    new_blk = pl.BlockSpec((sq, HEAD_W), lambda b, h: (b, h))
    return pl.pallas_call(
        kern,
        grid=(bsz, DIFF_H),
        in_specs=[pl.BlockSpec((sq, HEAD_W), lambda b, h: (b, q_col0 + h)),
                  past_blk, past_blk, new_blk, new_blk, lam_blk, lam_blk, lam_blk, lam_blk,
                  pl.BlockSpec((1, HEAD_W), lambda b, h: (0, 0))],
        out_specs=pl.BlockSpec((sq, HEAD_W), lambda b, h: (b, h)),
        out_shape=jax.ShapeDtypeStruct((bsz * sq, DIFF_W), BF16),
        scratch_shapes=[pltpu.VMEM((2 * sq, HEAD_W), BF16)],
        compiler_params=_params("parallel", "parallel"),
        name="full_attn",
    )(slab, k_past, v_past, kb, vb, *lam_p, subln_g)


def _merge_kernel(x_ref, c_ref, on_ref, mq_ref, gate_ref, mk_ref, mv_ref,
                  wco_ref, wdo_ref, wmo_ref, wout_ref, o_ref, om_ref, *, d):
    scale = MEM_DH ** -0.5
    for h in range(MEM_H):
        cols = slice(h * MEM_DH, (h + 1) * MEM_DH)
        s = lax.dot_general(mq_ref[:, cols], mk_ref[:, cols], (((1,), (1,)), ((), ())),
                            preferred_element_type=F32) * scale
        p = jnp.exp(s - jnp.max(s, axis=-1, keepdims=True))
        p = p / jnp.sum(p, axis=-1, keepdims=True)
        om_ref[:, cols] = jnp.dot(p.astype(BF16), mv_ref[:, cols],
                                  preferred_element_type=F32).astype(BF16)

    def gate(b):
        return jax.nn.sigmoid(gate_ref[:, b * d:(b + 1) * d].astype(F32))

    mix = gate(0) * jnp.dot(c_ref[...], wco_ref[...], preferred_element_type=F32)
    mix = mix + gate(1) * jnp.dot(on_ref[...], wdo_ref[...], preferred_element_type=F32)
    mix = mix + gate(2) * jnp.dot(om_ref[...], wmo_ref[...], preferred_element_type=F32)
    o_ref[...] = x_ref[...] + jnp.dot(mix.astype(BF16), wout_ref[...], preferred_element_type=F32)


def _merge(x, c, on, slab, mq_blk, mk, mv, wco, wdo, wmo, wout, bsz, seq):
    t, d = x.shape
    cc = c.shape[1]
    mlen = mk.shape[1]
    tm = _pick(seq, (256, 128, 64, 32))
    nt = seq // tm
    row = lambda i: (i, 0)
    full = lambda a: pl.BlockSpec(a.shape, lambda i: (0, 0))
    mem_blk = pl.BlockSpec((None, mlen, MEM_W), lambda i: (i // nt, 0, 0))
    return pl.pallas_call(
        functools.partial(_merge_kernel, d=d),
        grid=(t // tm,),
        in_specs=[pl.BlockSpec((tm, d), row), pl.BlockSpec((tm, cc), row),
                  pl.BlockSpec((tm, DIFF_W), row),
                  pl.BlockSpec((tm, MEM_W), lambda i: (i, mq_blk)),
                  pl.BlockSpec((tm, N_BRANCH * d), row),
                  mem_blk, mem_blk, full(wco), full(wdo), full(wmo), full(wout)],
        out_specs=pl.BlockSpec((tm, d), row),
        out_shape=jax.ShapeDtypeStruct((t, d), F32),
        scratch_shapes=[pltpu.VMEM((tm, MEM_W), BF16)],
        compiler_params=_params("parallel"),
        name="merge",
    )(x, c, on, slab, slab, mk, mv, wco, wdo, wmo, wout)


def _ffn_body(x_ref, g_ref, wg_ref, wu_ref, wd_ref, dw_ref, db_ref, ctx_ref, fin_ref,
              o_ref, st_ref, h_ref, ext_ref, tail_ref, acta_ref, actb_ref,
              *, tm, tf, nt, nf, final_norm):
    i = pl.program_id(0)
    j = pl.program_id(1)
    s = i % nt

    def gate_up(act_ref):
        col = pl.ds(pl.multiple_of(j * tf, tf), tf)
        h = h_ref[...]
        fg = jnp.dot(h, wg_ref[...], preferred_element_type=F32)
        fu = jnp.dot(h, wu_ref[...], preferred_element_type=F32)
        ext_ref[0:STATE_PAD, :] = jnp.where(s == 0, ctx_ref[:, col], tail_ref[j])
        ext_ref[STATE_PAD:STATE_PAD + tm, :] = fg
        last = fg[tm - STATE_PAD:tm, :]
        tail_ref[j] = last
        st_ref[:, col] = last
        y = (dw_ref[0:1, col] * ext_ref[STATE_PAD - 2:STATE_PAD - 2 + tm, :]
             + dw_ref[1:2, col] * ext_ref[STATE_PAD - 1:STATE_PAD - 1 + tm, :]
             + dw_ref[2:3, col] * fg + db_ref[:, col])
        act_ref[...] = (y * jax.nn.sigmoid(y) * fu).astype(BF16)

    def down(act_ref):
        o_ref[...] += jnp.dot(act_ref[...], wd_ref[...], preferred_element_type=F32)

    slots = (acta_ref, actb_ref)

    @pl.when(j == 0)
    def _():
        @pl.when(i == 0)
        def _():
            tail_ref[...] = jnp.zeros(tail_ref.shape, F32)

        h_ref[...] = _rms_rows(x_ref[...], g_ref[...]).astype(BF16)
        o_ref[...] = jnp.zeros(o_ref.shape, F32)
        gate_up(slots[0])

    for parity in (0, 1):
        @pl.when((j > 0) & (j < nf) & (j % 2 == parity))
        def _():
            down(slots[1 - parity])
            gate_up(slots[parity])

    @pl.when(j == nf)
    def _():
        down(slots[(nf - 1) % 2])
        out = x_ref[...] + o_ref[...]
        if final_norm:
            out = _rms_rows(out, fin_ref[...])
        o_ref[...] = out


def _ffn(x, g, wgu, wd, dw_w, dw_b, ctx, fin_g, bsz, seq, final_norm):
    t, d = x.shape
    ff = wd.shape[0]
    assert dw_w.shape[0] == 3 and ctx.shape == (bsz, STATE_PAD, ff)
    tm = _pick(seq, (1024, 512, 256, 128, 64, 32))
    tf = _pick(ff, (512, 256, 128))
    nt, nf = seq // tm, ff // tf
    assert nf >= 2
    kern = functools.partial(_ffn_body, tm=tm, tf=tf, nt=nt, nf=nf, final_norm=final_norm)
    cur = lambda j: jnp.minimum(j, nf - 1)
    prev = lambda j: jnp.maximum(j - 1, 0)
    whole = lambda a: pl.BlockSpec(a.shape, lambda i, j: (0, 0))
    return pl.pallas_call(
        kern,
        grid=(t // tm, nf + 1),
        in_specs=[pl.BlockSpec((tm, d), lambda i, j: (i, 0), pipeline_mode=pl.Buffered(1)),
                  whole(g),
                  pl.BlockSpec((d, tf), lambda i, j: (0, cur(j))),
                  pl.BlockSpec((d, tf), lambda i, j: (0, nf + cur(j))),
                  pl.BlockSpec((tf, d), lambda i, j: (prev(j), 0)),
                  whole(dw_w), whole(dw_b),
                  pl.BlockSpec((None, STATE_PAD, ff), lambda i, j: (i // nt, 0, 0)),
                  whole(fin_g)],
        out_specs=[pl.BlockSpec((tm, d), lambda i, j: (i, 0)),
                   pl.BlockSpec((None, STATE_PAD, ff), lambda i, j: (i // nt, 0, 0))],
        out_shape=[jax.ShapeDtypeStruct((t, d), F32),
                   jax.ShapeDtypeStruct((bsz, STATE_PAD, ff), F32)],
        scratch_shapes=[pltpu.VMEM((tm, d), BF16),
                        pltpu.VMEM((STATE_PAD + tm, tf), F32),
                        pltpu.VMEM((nf, STATE_PAD, tf), F32),
                        pltpu.VMEM((tm, tf), BF16), pltpu.VMEM((tm, tf), BF16)],
        compiler_params=_params("arbitrary", "arbitrary"),
        name="conv_ffn",
    )(x, g, wgu, wgu, wd, dw_w, dw_b, ctx, fin_g)


def _row(v):
    return v.reshape(1, -1)


def _pad_state(st, rows):
    return jnp.pad(st, ((0, 0), (rows - st.shape[1], 0), (0, 0)))


def _layer(x, bsz, seq, layer_idx, conv_ctx, ffn_ctx, k_past, v_past, mk, mv, P, final_g, final_norm):
    t, d = x.shape
    cc = P["conv_dw_w"].shape[1]
    l_init = _lambda_init(layer_idx)

    slab = _norm_proj(x, P["norm1_g"], P["w_main"], BF16)
    gate_w = N_BRANCH * d
    conv_blk = gate_w // (2 * cc)
    q_col0 = (gate_w + 2 * cc) // HEAD_W
    mq_blk = (gate_w + 2 * cc + DIFF_W) // MEM_W
    assert gate_w % (2 * cc) == 0 and (gate_w + 2 * cc + DIFF_W) % MEM_W == 0

    c, conv_state = _conv_branch(slab, conv_blk, conv_ctx, P["conv_dw_w"], P["conv_dw_b"],
                                 P["conv_ln_g"], P["conv_ln_b"], bsz, seq)

    if k_past is None:
        k, v, kb, vt = _kv_proj_vt(x, P["norm1_g"], P["w_kv"], bsz, seq)
        on = _causal_attn(slab, q_col0, kb.reshape(bsz, seq, DIFF_W), vt, P["lam"],
                          P["diff_subln_g"], bsz, seq, l_init)
    else:
        k, v, kb, vb = _kv_proj(x, P["norm1_g"], P["w_kv"])
        on = _full_attn(slab, q_col0, k_past, v_past, kb, vb, P["lam"],
                        P["diff_subln_g"], bsz, seq, l_init)

    x1 = _merge(x, c, on, slab, mq_blk, mk, mv, P["w_conv_out"], P["w_diff_out"],
                P["w_mem_out"], P["w_out"], bsz, seq)
    x2, ffn_state = _ffn(x1, P["norm2_g"], P["w_ffn_gu"], P["w_ffn_down"], P["ffn_dw_w"],
                         P["ffn_dw_b"], ffn_ctx, final_g, bsz, seq, final_norm)
    return x2, k, v, conv_state, ffn_state


def kernel(x_prompt, x_sample, mem_prompt, cache_k, cache_v, cache_mem_k, cache_mem_v, state_conv, state_ffn_conv, norm1_g, w_in, conv_dw_w, conv_dw_b, conv_ln_g, conv_ln_b, w_conv_out, lam_q1, lam_k1, lam_q2, lam_k2, diff_subln_g, w_diff_out, mem_norm_g, w_mem_kv, w_mem_out, w_out, norm2_g, w_ffn_gu, ffn_dw_w, ffn_dw_b, w_ffn_down, final_g):
    depth = w_in.shape[0]
    bp, sp, d = x_prompt.shape
    bs, ss, _ = x_sample.shape
    mlen = mem_prompt.shape[1]
    ktaps, cc = conv_dw_w.shape[1:]
    ff = w_ffn_down.shape[1]
    past = cache_k.shape[2]
    conv_hist = 32

    xp = x_prompt.reshape(bp * sp, d)
    xs = x_sample.reshape(bs * ss, d)
    mem = mem_prompt.reshape(bp * mlen, d)
    fin = _row(final_g)

    c0, c1, c2, c3, c4 = (2 * cc, 2 * cc + DIFF_W, 2 * cc + 2 * DIFF_W, 2 * cc + 3 * DIFF_W,
                          2 * cc + 3 * DIFF_W + MEM_W)
    outs = {n: [] for n in ("kp", "vp", "mkp", "mvp", "cp", "fp", "ks", "vs", "cs", "fs")}
    for l in range(depth):
        wl = w_in[l].astype(BF16)
        P = {
            "norm1_g": _row(norm1_g[l]),
            "w_kv": wl[:, c1:c3],
            "w_main": jnp.concatenate([wl[:, c4:], wl[:, :c0], wl[:, c0:c1], wl[:, c3:c4]], axis=1),
            "conv_dw_w": conv_dw_w[l], "conv_dw_b": _row(conv_dw_b[l]),
            "conv_ln_g": _row(conv_ln_g[l]), "conv_ln_b": _row(conv_ln_b[l]),
            "w_conv_out": w_conv_out[l].astype(BF16),
            "lam": (_row(lam_q1[l]), _row(lam_k1[l]), _row(lam_q2[l]), _row(lam_k2[l])),
            "diff_subln_g": _row(diff_subln_g[l]),
            "w_diff_out": w_diff_out[l].astype(BF16),
            "w_mem_out": w_mem_out[l].astype(BF16),
            "w_out": w_out[l].astype(BF16),
            "norm2_g": _row(norm2_g[l]),
            "w_ffn_gu": w_ffn_gu[l].astype(BF16),
            "ffn_dw_w": ffn_dw_w[l], "ffn_dw_b": _row(ffn_dw_b[l]),
            "w_ffn_down": w_ffn_down[l].astype(BF16),
        }
        final_norm = l == depth - 1

        mkv = _norm_proj(mem, _row(mem_norm_g[l]), w_mem_kv[l].astype(BF16), F32)
        mk_p = mkv[:, :MEM_W].reshape(bp, mlen, MEM_W)
        mv_p = mkv[:, MEM_W:].reshape(bp, mlen, MEM_W)
        xp, kp, vp, cp, fp = _layer(
            xp, bp, sp, l, jnp.zeros((bp, conv_hist, cc), F32), jnp.zeros((bp, STATE_PAD, ff), F32),
            None, None, mk_p.astype(BF16), mv_p.astype(BF16), P, fin, final_norm)
        outs["kp"].append(kp.reshape(bp, sp, DIFF_H, HEAD_W))
        outs["vp"].append(vp.reshape(bp, sp, DIFF_H, HEAD_W))
        outs["mkp"].append(mk_p.reshape(bp, mlen, MEM_H, MEM_DH))
        outs["mvp"].append(mv_p.reshape(bp, mlen, MEM_H, MEM_DH))
        outs["cp"].append(cp[:, conv_hist - (ktaps - 1):])
        outs["fp"].append(fp[:, STATE_PAD - 2:])

        xs, ks_, vs_, cs_, fs_ = _layer(
            xs, bs, ss, l, _pad_state(state_conv[l], conv_hist), _pad_state(state_ffn_conv[l], STATE_PAD),
            cache_k[l].reshape(bs, past, DIFF_W), cache_v[l].reshape(bs, past, DIFF_W),
            cache_mem_k[l].reshape(bs, mlen, MEM_W).astype(BF16),
            cache_mem_v[l].reshape(bs, mlen, MEM_W).astype(BF16), P, fin, final_norm)
        outs["ks"].append(ks_.reshape(bs, ss, DIFF_H, HEAD_W))
        outs["vs"].append(vs_.reshape(bs, ss, DIFF_H, HEAD_W))
        outs["cs"].append(cs_[:, conv_hist - (ktaps - 1):])
        outs["fs"].append(fs_[:, STATE_PAD - 2:])

    st = lambda n: jnp.stack(outs[n])
    return (xp.reshape(bp, sp, d), xs.reshape(bs, ss, d),
            st("kp"), st("vp"), st("mkp"), st("mvp"), st("cp"), st("fp"),
            st("ks"), st("vs"), st("cs"), st("fs"))
```

```python
import functools
import math

import jax
import jax.numpy as jnp
from jax import lax
from jax.experimental import pallas as pl
from jax.experimental.pallas import tpu as pltpu

F32 = jnp.float32
BF16 = jnp.bfloat16

EPS = 1e-6
CHUNK = 64
DIFF_H = 8
DIFF_DH = 64
HEAD_W = 2 * DIFF_DH
DIFF_W = DIFF_H * HEAD_W
MEM_H = 4
MEM_DH = 128
MEM_W = MEM_H * MEM_DH
N_BRANCH = 3
STATE_PAD = 8

VMEM_LIMIT_BYTES = 56 * 1024 * 1024


def _lambda_init(layer_idx):
    return 0.8 - 0.6 * math.exp(-0.3 * layer_idx)


def _params(*sem):
    return pltpu.CompilerParams(dimension_semantics=sem, vmem_limit_bytes=VMEM_LIMIT_BYTES)


def _pick(n, prefs):
    for p in prefs:
        if n % p == 0:
            return p
    return n


def _rms_rows(x, g):
    return x * lax.rsqrt(jnp.mean(x * x, axis=-1, keepdims=True) + EPS) * g


def _kv_proj_kernel(x_ref, g_ref, w_ref, k_ref, v_ref, kb_ref, vb_ref):
    h = _rms_rows(x_ref[...], g_ref[...]).astype(BF16)
    kv = jnp.dot(h, w_ref[...], preferred_element_type=F32)
    k = kv[:, :DIFF_W]
    v = kv[:, DIFF_W:]
    k_ref[...] = k
    v_ref[...] = v
    kb_ref[...] = k.astype(BF16)
    vb_ref[...] = v.astype(BF16)


def _kv_proj(x, g, w_in, kv_blk):
    t, d = x.shape
    tm = _pick(t, (512, 256))
    row = lambda i: (i, 0)
    const = lambda i: (0, 0)
    out_blk = pl.BlockSpec((tm, DIFF_W), row)
    return pl.pallas_call(
        _kv_proj_kernel,
        grid=(t // tm,),
        in_specs=[pl.BlockSpec((tm, d), row), pl.BlockSpec((1, d), const),
                  pl.BlockSpec((d, 2 * DIFF_W), lambda i: (0, kv_blk))],
        out_specs=[out_blk, out_blk, out_blk, out_blk],
        out_shape=[jax.ShapeDtypeStruct((t, DIFF_W), F32), jax.ShapeDtypeStruct((t, DIFF_W), F32),
                   jax.ShapeDtypeStruct((t, DIFF_W), BF16), jax.ShapeDtypeStruct((t, DIFF_W), BF16)],
        compiler_params=_params("parallel"),
        name="kv_proj",
    )(x, g, w_in)


ATT_BLK = 512
VT_ROWS = HEAD_W + 16


def _kv_proj_vt_kernel(x_ref, g_ref, w_ref, k_ref, v_ref, kb_ref, vt_ref):
    h = _rms_rows(x_ref[...], g_ref[...]).astype(BF16)
    kv = jnp.dot(h, w_ref[...], preferred_element_type=F32)
    k = kv[:, :DIFF_W]
    v = kv[:, DIFF_W:]
    k_ref[...] = k
    v_ref[...] = v
    kb_ref[...] = k.astype(BF16)
    tm = v.shape[0]
    for hd in range(DIFF_H):
        vt_ref[hd, 0:HEAD_W, :] = v[:, hd * HEAD_W:(hd + 1) * HEAD_W].T.astype(BF16)
        vt_ref[hd, HEAD_W:VT_ROWS, :] = jnp.ones((VT_ROWS - HEAD_W, tm), BF16)


def _kv_proj_vt(x, g, w_in, kv_blk, bsz, seq):
    t, d = x.shape
    tm = ATT_BLK
    nb = seq // tm
    row = lambda i: (i, 0)
    const = lambda i: (0, 0)
    out_blk = pl.BlockSpec((tm, DIFF_W), row)
    return pl.pallas_call(
        _kv_proj_vt_kernel,
        grid=(t // tm,),
        in_specs=[pl.BlockSpec((tm, d), row), pl.BlockSpec((1, d), const),
                  pl.BlockSpec((d, 2 * DIFF_W), lambda i: (0, kv_blk))],
        out_specs=[out_blk, out_blk, out_blk,
                   pl.BlockSpec((None, DIFF_H, None, VT_ROWS, tm),
                                lambda i: (i // nb, 0, i % nb, 0, 0))],
        out_shape=[jax.ShapeDtypeStruct((t, DIFF_W), F32), jax.ShapeDtypeStruct((t, DIFF_W), F32),
                   jax.ShapeDtypeStruct((t, DIFF_W), BF16),
                   jax.ShapeDtypeStruct((bsz, DIFF_H, nb, VT_ROWS, tm), BF16)],
        compiler_params=_params("parallel"),
        name="kv_proj_vt",
    )(x, g, w_in)


def _norm_proj_kernel(x_ref, g_ref, w_ref, o_ref, h_ref):
    @pl.when(pl.program_id(1) == 0)
    def _():
        h_ref[...] = _rms_rows(x_ref[...], g_ref[...]).astype(BF16)

    o_ref[...] = jnp.dot(h_ref[...], w_ref[...], preferred_element_type=F32).astype(o_ref.dtype)


PROJ_TN = 512


def _norm_proj(x, g, w, out_dtype, col_tiles=None):
    t, d = x.shape
    tm = _pick(t, (1024, 512, 256))
    if col_tiles is None:
        n = w.shape[1]
        tn = _pick(n, (PROJ_TN, 256, 128))
        col_tiles = (n // tn, lambda j: j)
    else:
        tn = PROJ_TN
    nn, col_map = col_tiles
    n = nn * tn
    return pl.pallas_call(
        _norm_proj_kernel,
        grid=(t // tm, nn),
        in_specs=[pl.BlockSpec((tm, d), lambda i, j: (i, 0)),
                  pl.BlockSpec((1, d), lambda i, j: (0, 0)),
                  pl.BlockSpec((d, tn), lambda i, j: (0, col_map(j)))],
        out_specs=pl.BlockSpec((tm, tn), lambda i, j: (i, j)),
        out_shape=jax.ShapeDtypeStruct((t, n), out_dtype),
        scratch_shapes=[pltpu.VMEM((tm, d), BF16)],
        compiler_params=_params("parallel", "arbitrary"),
        name="norm_proj",
    )(x, g, w)


CONV_ROWS = 32


def _conv_branch_kernel(cin_ref, ctx_ref, w_ref, b_ref, lg_ref, lb_ref, c_ref, st_ref, ubuf, sh_ref,
                        *, tm, cc, ktaps, hist):
    s = pl.program_id(1)
    lead = hist - (ktaps - 1)
    sub = STATE_PAD

    @pl.when(s == 0)
    def _():
        ubuf[0:hist, :] = ctx_ref[...]

    cin = cin_ref[...].astype(F32)
    u = cin[:, :cc] * jax.nn.sigmoid(cin[:, cc:])
    ubuf[hist:hist + tm, :] = u
    span = hist + tm - sub
    for r in range(1, sub):
        sh_ref[r - 1, 0:span, :] = ubuf[r:r + span, :]

    bias = b_ref[...]
    lg = lg_ref[...]
    lb = lb_ref[...]
    for r0 in range(0, tm, CONV_ROWS):
        acc = jnp.zeros((CONV_ROWS, cc), F32) + bias
        for k in range(ktaps):
            r = (lead + k) % sub
            a = lead + k - r + r0
            if r == 0:
                tap = ubuf[a:a + CONV_ROWS, :]
            else:
                tap = sh_ref[r - 1, a:a + CONV_ROWS, :]
            acc = acc + w_ref[k:k + 1, :] * tap
        mu = jnp.mean(acc, axis=-1, keepdims=True)
        xc = acc - mu
        var = jnp.mean(xc * xc, axis=-1, keepdims=True)
        y = xc * lax.rsqrt(var + EPS) * lg + lb
        c_ref[r0:r0 + CONV_ROWS, :] = (y * jax.nn.sigmoid(y)).astype(c_ref.dtype)

    tail = ubuf[tm:tm + hist, :]
    st_ref[...] = tail
    ubuf[0:hist, :] = tail


def _conv_branch(slab, col_blk, ctx, dw_w, dw_b, ln_g, ln_b, bsz, seq):
    ktaps, cc = dw_w.shape
    hist = 32
    assert ktaps - 1 <= hist and ctx.shape == (bsz, hist, cc)
    tm = _pick(seq, (512, 256, 128, 64, 32))
    nt = seq // tm
    kern = functools.partial(_conv_branch_kernel, tm=tm, cc=cc, ktaps=ktaps, hist=hist)
    vec = pl.BlockSpec((1, cc), lambda b, s: (0, 0))
    return pl.pallas_call(
        kern,
        grid=(bsz, nt),
        in_specs=[pl.BlockSpec((tm, 2 * cc), lambda b, s: (b * nt + s, col_blk)),
                  pl.BlockSpec((None, hist, cc), lambda b, s: (b, 0, 0)),
                  pl.BlockSpec((ktaps, cc), lambda b, s: (0, 0)),
                  vec, vec, vec],
        out_specs=[pl.BlockSpec((tm, cc), lambda b, s: (b * nt + s, 0)),
                   pl.BlockSpec((None, hist, cc), lambda b, s: (b, 0, 0))],
        out_shape=[jax.ShapeDtypeStruct((bsz * seq, cc), BF16),
                   jax.ShapeDtypeStruct((bsz, hist, cc), F32)],
        scratch_shapes=[pltpu.VMEM((hist + tm, cc), F32),
                        pltpu.VMEM((STATE_PAD - 1, hist + tm - STATE_PAD, cc), F32)],
        compiler_params=_params("arbitrary", "arbitrary"),
        name="conv_branch",
    )(slab, ctx, dw_w, dw_b, ln_g, ln_b)


def _full_attn_kernel(q_ref, kp_ref, vp_ref, kn_ref, vn_ref, lq1_ref, lk1_ref, lq2_ref, lk2_ref,
                      g_ref, o_ref, qq_ref, *, bq, l_init):
    lane = lax.broadcasted_iota(jnp.int32, (bq, HEAD_W), 1)
    q = q_ref[...] * (DIFF_DH ** -0.5)
    zero = jnp.zeros_like(q)
    qq_ref[0:bq, :] = jnp.where(lane < DIFF_DH, q, zero)
    qq_ref[bq:2 * bq, :] = jnp.where(lane >= DIFF_DH, q, zero)
    nt_dims = (((1,), (1,)), ((), ()))
    qq = qq_ref[...]
    s_p = lax.dot_general(qq, kp_ref[...].astype(BF16), nt_dims, preferred_element_type=F32)
    s_n = lax.dot_general(qq, kn_ref[...], nt_dims, preferred_element_type=F32)
    m = jnp.maximum(jnp.max(s_p, axis=-1, keepdims=True), jnp.max(s_n, axis=-1, keepdims=True))
    p_p = jnp.exp(s_p - m)
    p_n = jnp.exp(s_n - m)
    l = jnp.sum(p_p, axis=-1, keepdims=True) + jnp.sum(p_n, axis=-1, keepdims=True)
    o_all = (jnp.dot(p_p.astype(BF16), vp_ref[...].astype(BF16), preferred_element_type=F32)
             + jnp.dot(p_n.astype(BF16), vn_ref[...], preferred_element_type=F32)) / l
    lam = (jnp.exp(jnp.sum(lq1_ref[...] * lk1_ref[...], axis=-1, keepdims=True))
           - jnp.exp(jnp.sum(lq2_ref[...] * lk2_ref[...], axis=-1, keepdims=True)) + l_init)
    o = o_all[0:bq, :] - lam * o_all[bq:2 * bq, :]
    o_ref[...] = (_rms_rows(o, g_ref[...]) * (1.0 - l_init)).astype(o_ref.dtype)


def _causal_attn_kernel(q_ref, k_ref, vt_ref, lq1_ref, lk1_ref, lq2_ref, lk2_ref, g_ref, o_ref,
                        qq_ref, sa_ref, sb_ref, ma_ref, mb_ref, m_ref, acc_ref, *, blk, l_init):
    i = pl.program_id(2)
    lane = lax.broadcasted_iota(jnp.int32, (blk, HEAD_W), 1)
    q = q_ref[...] * (DIFF_DH ** -0.5)
    zero = jnp.zeros_like(q)
    qq_ref[0:blk, :] = jnp.where(lane < DIFF_DH, q, zero)
    qq_ref[blk:2 * blk, :] = jnp.where(lane >= DIFF_DH, q, zero)
    m_ref[...] = jnp.full(m_ref.shape, -jnp.inf, F32)
    acc_ref[...] = jnp.zeros(acc_ref.shape, F32)

    def scores(j, s_ref, mx_ref):
        start = pl.multiple_of(j * blk, blk)
        st = lax.dot_general(k_ref[pl.ds(start, blk), :], qq_ref[...], (((1,), (1,)), ((), ())),
                             preferred_element_type=F32)
        s_ref[...] = st
        mx_ref[...] = jnp.max(st, axis=0, keepdims=True)

    def update(j, s_ref, mx_ref, masked):
        st = s_ref[...]
        if masked:
            kchunk = lax.broadcasted_iota(jnp.int32, (blk, 1), 0) // CHUNK
            qcol = lax.broadcasted_iota(jnp.int32, (1, 2 * blk), 1)
            qchunk = jnp.where(qcol >= blk, qcol - blk, qcol) // CHUNK
            st = jnp.where(kchunk <= qchunk, st, -jnp.inf)
            mx = jnp.max(st, axis=0, keepdims=True)
        else:
            mx = mx_ref[...]
        m_prev = m_ref[...]
        m_new = jnp.maximum(m_prev, mx)
        alpha = jnp.exp(m_prev - m_new)
        pt = jnp.exp(st - m_new).astype(BF16)
        acc_ref[...] = alpha * acc_ref[...] + jnp.dot(vt_ref[j], pt, preferred_element_type=F32)
        m_ref[...] = m_new

    scores(0, sa_ref, ma_ref)

    def pair(jp, carry):
        j = 2 * jp
        scores(j + 1, sb_ref, mb_ref)
        update(j, sa_ref, ma_ref, False)
        scores(j + 2, sa_ref, ma_ref)
        update(j + 1, sb_ref, mb_ref, False)
        return carry

    lax.fori_loop(0, i // 2, pair, 0)

    @pl.when(i % 2 == 1)
    def _():
        scores(i, sb_ref, mb_ref)
        update(i - 1, sa_ref, ma_ref, False)
        update(i, sb_ref, mb_ref, True)

    @pl.when(i % 2 == 0)
    def _():
        update(i, sa_ref, ma_ref, True)

    lam = (jnp.exp(jnp.sum(lq1_ref[...] * lk1_ref[...], axis=-1, keepdims=True))
           - jnp.exp(jnp.sum(lq2_ref[...] * lk2_ref[...], axis=-1, keepdims=True)) + l_init)
    acc = acc_ref[...]
    o_all = acc[0:HEAD_W, :] / acc[HEAD_W:HEAD_W + 1, :]
    ot = o_all[:, 0:blk] - lam * o_all[:, blk:2 * blk]
    ot = ot * lax.rsqrt(jnp.mean(ot * ot, axis=0, keepdims=True) + EPS) * (1.0 - l_init)
    o_ref[...] = (ot.T * g_ref[...]).astype(o_ref.dtype)


def _causal_attn(slab, q_col0, kb, vt, lam_p, subln_g, bsz, seq, l_init):
    blk = ATT_BLK
    assert seq % blk == 0 and blk % CHUNK == 0
    nq = seq // blk
    kern = functools.partial(_causal_attn_kernel, blk=blk, l_init=l_init)
    lam_blk = pl.BlockSpec((1, DIFF_DH), lambda b, h, i: (0, 0))
    return pl.pallas_call(
        kern,
        grid=(bsz, DIFF_H, nq),
        in_specs=[pl.BlockSpec((blk, HEAD_W), lambda b, h, i: (b * nq + i, q_col0 + h)),
                  pl.BlockSpec((None, seq, HEAD_W), lambda b, h, i: (b, 0, h)),
                  pl.BlockSpec((None, None, nq, VT_ROWS, blk), lambda b, h, i: (b, h, 0, 0, 0)),
                  lam_blk, lam_blk, lam_blk, lam_blk,
                  pl.BlockSpec((1, HEAD_W), lambda b, h, i: (0, 0))],
        out_specs=pl.BlockSpec((blk, HEAD_W), lambda b, h, i: (b * nq + i, h)),
        out_shape=jax.ShapeDtypeStruct((bsz * seq, DIFF_W), BF16),
        scratch_shapes=[pltpu.VMEM((2 * blk, HEAD_W), BF16),
                        pltpu.VMEM((blk, 2 * blk), F32), pltpu.VMEM((blk, 2 * blk), F32),
                        pltpu.VMEM((1, 2 * blk), F32), pltpu.VMEM((1, 2 * blk), F32),
                        pltpu.VMEM((1, 2 * blk), F32),
                        pltpu.VMEM((VT_ROWS, 2 * blk), F32)],
        compiler_params=_params("parallel", "parallel", "arbitrary"),
        name="causal_attn",
    )(slab, kb, vt, *lam_p, subln_g)


def _full_attn(slab, q_col0, k_past, v_past, kb, vb, lam_p, subln_g, bsz, sq, l_init):
    past = k_past.shape[1]
    kern = functools.partial(_full_attn_kernel, bq=sq, l_init=l_init)
    lam_blk = pl.BlockSpec((1, DIFF_DH), lambda b, h: (0, 0))
    past_blk = pl.BlockSpec((None, past, HEAD_W), lambda b, h: (b, 0, h))
    new_blk = pl.BlockSpec((sq, HEAD_W), lambda b, h: (b, h))
    return pl.pallas_call(
        kern,
        grid=(bsz, DIFF_H),
        in_specs=[pl.BlockSpec((sq, HEAD_W), lambda b, h: (b, q_col0 + h)),
                  past_blk, past_blk, new_blk, new_blk, lam_blk, lam_blk, lam_blk, lam_blk,
                  pl.BlockSpec((1, HEAD_W), lambda b, h: (0, 0))],
        out_specs=pl.BlockSpec((sq, HEAD_W), lambda b, h: (b, h)),
        out_shape=jax.ShapeDtypeStruct((bsz * sq, DIFF_W), BF16),
        scratch_shapes=[pltpu.VMEM((2 * sq, HEAD_W), BF16)],
        compiler_params=_params("parallel", "parallel"),
        name="full_attn",
    )(slab, k_past, v_past, kb, vb, *lam_p, subln_g)


def _merge_kernel(x_ref, c_ref, on_ref, mq_ref, gate_ref, mk_ref, mv_ref,
                  wco_ref, wdo_ref, wmo_ref, wout_ref, o_ref, om_ref, *, d):
    scale = MEM_DH ** -0.5
    for h in range(MEM_H):
        cols = slice(h * MEM_DH, (h + 1) * MEM_DH)
        s = lax.dot_general(mq_ref[:, cols], mk_ref[:, cols], (((1,), (1,)), ((), ())),
                            preferred_element_type=F32) * scale
        p = jnp.exp(s - jnp.max(s, axis=-1, keepdims=True))
        p = p / jnp.sum(p, axis=-1, keepdims=True)
        om_ref[:, cols] = jnp.dot(p.astype(BF16), mv_ref[:, cols],
                                  preferred_element_type=F32).astype(BF16)

    def gate(b):
        return jax.nn.sigmoid(gate_ref[:, b * d:(b + 1) * d].astype(F32))

    mix = gate(0) * jnp.dot(c_ref[...], wco_ref[...], preferred_element_type=F32)
    mix = mix + gate(1) * jnp.dot(on_ref[...], wdo_ref[...], preferred_element_type=F32)
    mix = mix + gate(2) * jnp.dot(om_ref[...], wmo_ref[...], preferred_element_type=F32)
    o_ref[...] = x_ref[...] + jnp.dot(mix.astype(BF16), wout_ref[...], preferred_element_type=F32)


def _merge(x, c, on, slab, mq_blk, mk, mv, wco, wdo, wmo, wout, bsz, seq):
    t, d = x.shape
    cc = c.shape[1]
    mlen = mk.shape[1]
    tm = _pick(seq, (256, 128, 64, 32))
    nt = seq // tm
    row = lambda i: (i, 0)
    full = lambda a: pl.BlockSpec(a.shape, lambda i: (0, 0))
    mem_blk = pl.BlockSpec((None, mlen, MEM_W), lambda i: (i // nt, 0, 0))
    return pl.pallas_call(
        functools.partial(_merge_kernel, d=d),
        grid=(t // tm,),
        in_specs=[pl.BlockSpec((tm, d), row), pl.BlockSpec((tm, cc), row),
                  pl.BlockSpec((tm, DIFF_W), row),
                  pl.BlockSpec((tm, MEM_W), lambda i: (i, mq_blk)),
                  pl.BlockSpec((tm, N_BRANCH * d), row),
                  mem_blk, mem_blk, full(wco), full(wdo), full(wmo), full(wout)],
        out_specs=pl.BlockSpec((tm, d), row),
        out_shape=jax.ShapeDtypeStruct((t, d), F32),
        scratch_shapes=[pltpu.VMEM((tm, MEM_W), BF16)],
        compiler_params=_params("parallel"),
        name="merge",
    )(x, c, on, slab, slab, mk, mv, wco, wdo, wmo, wout)


def _ffn_body(x_ref, g_ref, wg_ref, wu_ref, wd_ref, dw_ref, db_ref, ctx_ref, fin_ref,
              o_ref, st_ref, h_ref, ext_ref, tail_ref, acta_ref, actb_ref,
              *, tm, tf, nt, nf, final_norm):
    i = pl.program_id(0)
    j = pl.program_id(1)
    s = i % nt

    def gate_up(act_ref):
        col = pl.ds(pl.multiple_of(j * tf, tf), tf)
        h = h_ref[...]
        fg = jnp.dot(h, wg_ref[...], preferred_element_type=F32)
        fu = jnp.dot(h, wu_ref[...], preferred_element_type=F32)
        ext_ref[0:STATE_PAD, :] = jnp.where(s == 0, ctx_ref[:, col], tail_ref[j])
        ext_ref[STATE_PAD:STATE_PAD + tm, :] = fg
        last = fg[tm - STATE_PAD:tm, :]
        tail_ref[j] = last
        st_ref[:, col] = last
        y = (dw_ref[0:1, col] * ext_ref[STATE_PAD - 2:STATE_PAD - 2 + tm, :]
             + dw_ref[1:2, col] * ext_ref[STATE_PAD - 1:STATE_PAD - 1 + tm, :]
             + dw_ref[2:3, col] * fg + db_ref[:, col])
        act_ref[...] = (y * jax.nn.sigmoid(y) * fu).astype(BF16)

    def down(act_ref):
        o_ref[...] += jnp.dot(act_ref[...], wd_ref[...], preferred_element_type=F32)

    slots = (acta_ref, actb_ref)

    @pl.when(j == 0)
    def _():
        @pl.when(i == 0)
        def _():
            tail_ref[...] = jnp.zeros(tail_ref.shape, F32)

        h_ref[...] = _rms_rows(x_ref[...], g_ref[...]).astype(BF16)
        o_ref[...] = jnp.zeros(o_ref.shape, F32)
        gate_up(slots[0])

    for parity in (0, 1):
        @pl.when((j > 0) & (j < nf) & (j % 2 == parity))
        def _():
            down(slots[1 - parity])
            gate_up(slots[parity])

    @pl.when(j == nf)
    def _():
        down(slots[(nf - 1) % 2])
        out = x_ref[...] + o_ref[...]
        if final_norm:
            out = _rms_rows(out, fin_ref[...])
        o_ref[...] = out


def _ffn(x, g, wgu, wd, dw_w, dw_b, ctx, fin_g, bsz, seq, final_norm):
    t, d = x.shape
    ff = wd.shape[0]
    assert dw_w.shape[0] == 3 and ctx.shape == (bsz, STATE_PAD, ff)
    tm = _pick(seq, (1024, 512, 256, 128, 64, 32))
    tf = _pick(ff, (512, 256, 128))
    nt, nf = seq // tm, ff // tf
    assert nf >= 2
    kern = functools.partial(_ffn_body, tm=tm, tf=tf, nt=nt, nf=nf, final_norm=final_norm)
    cur = lambda j: jnp.minimum(j, nf - 1)
    prev = lambda j: jnp.maximum(j - 1, 0)
    whole = lambda a: pl.BlockSpec(a.shape, lambda i, j: (0, 0))
    return pl.pallas_call(
        kern,
        grid=(t // tm, nf + 1),
        in_specs=[pl.BlockSpec((tm, d), lambda i, j: (i, 0), pipeline_mode=pl.Buffered(1)),
                  whole(g),
                  pl.BlockSpec((d, tf), lambda i, j: (0, cur(j))),
                  pl.BlockSpec((d, tf), lambda i, j: (0, nf + cur(j))),
                  pl.BlockSpec((tf, d), lambda i, j: (prev(j), 0)),
                  whole(dw_w), whole(dw_b),
                  pl.BlockSpec((None, STATE_PAD, ff), lambda i, j: (i // nt, 0, 0)),
                  whole(fin_g)],
        out_specs=[pl.BlockSpec((tm, d), lambda i, j: (i, 0)),
                   pl.BlockSpec((None, STATE_PAD, ff), lambda i, j: (i // nt, 0, 0))],
        out_shape=[jax.ShapeDtypeStruct((t, d), F32),
                   jax.ShapeDtypeStruct((bsz, STATE_PAD, ff), F32)],
        scratch_shapes=[pltpu.VMEM((tm, d), BF16),
                        pltpu.VMEM((STATE_PAD + tm, tf), F32),
                        pltpu.VMEM((nf, STATE_PAD, tf), F32),
                        pltpu.VMEM((tm, tf), BF16), pltpu.VMEM((tm, tf), BF16)],
        compiler_params=_params("arbitrary", "arbitrary"),
        name="conv_ffn",
    )(x, g, wgu, wgu, wd, dw_w, dw_b, ctx, fin_g)


def _row(v):
    return v.reshape(1, -1)


def _pad_state(st, rows):
    return jnp.pad(st, ((0, 0), (rows - st.shape[1], 0), (0, 0)))


def _layer(x, bsz, seq, layer_idx, conv_ctx, ffn_ctx, k_past, v_past, mk, mv, P, final_g, final_norm):
    t, d = x.shape
    cc = P["conv_dw_w"].shape[1]
    l_init = _lambda_init(layer_idx)

    gate_w = N_BRANCH * d
    front = 2 * cc + DIFF_W
    for edge in (front, 2 * DIFF_W, MEM_W, gate_w):
        assert edge % PROJ_TN == 0
    g_t, front_t = gate_w // PROJ_TN, front // PROJ_TN
    mq0 = (front + 2 * DIFF_W) // PROJ_TN
    gates0 = mq0 + MEM_W // PROJ_TN
    col_map = lambda j: jnp.where(j < g_t, j + gates0,
                                  jnp.where(j < g_t + front_t, j - g_t, j - g_t - front_t + mq0))
    slab = _norm_proj(x, P["norm1_g"], P["w_in"], BF16,
                      col_tiles=(g_t + front_t + MEM_W // PROJ_TN, col_map))
    conv_blk = gate_w // (2 * cc)
    q_col0 = (gate_w + 2 * cc) // HEAD_W
    mq_blk = (gate_w + 2 * cc + DIFF_W) // MEM_W
    assert gate_w % (2 * cc) == 0 and (gate_w + 2 * cc + DIFF_W) % MEM_W == 0
    assert front % (2 * DIFF_W) == 0
    kv_blk = front // (2 * DIFF_W)

    c, conv_state = _conv_branch(slab, conv_blk, conv_ctx, P["conv_dw_w"], P["conv_dw_b"],
                                 P["conv_ln_g"], P["conv_ln_b"], bsz, seq)

    if k_past is None:
        k, v, kb, vt = _kv_proj_vt(x, P["norm1_g"], P["w_in"], kv_blk, bsz, seq)
        on = _causal_attn(slab, q_col0, kb.reshape(bsz, seq, DIFF_W), vt, P["lam"],
                          P["diff_subln_g"], bsz, seq, l_init)
    else:
        k, v, kb, vb = _kv_proj(x, P["norm1_g"], P["w_in"], kv_blk)
        on = _full_attn(slab, q_col0, k_past, v_past, kb, vb, P["lam"],
                        P["diff_subln_g"], bsz, seq, l_init)

    x1 = _merge(x, c, on, slab, mq_blk, mk, mv, P["w_conv_out"], P["w_diff_out"],
                P["w_mem_out"], P["w_out"], bsz, seq)
    x2, ffn_state = _ffn(x1, P["norm2_g"], P["w_ffn_gu"], P["w_ffn_down"], P["ffn_dw_w"],
                         P["ffn_dw_b"], ffn_ctx, final_g, bsz, seq, final_norm)
    return x2, k, v, conv_state, ffn_state


def kernel(x_prompt, x_sample, mem_prompt, cache_k, cache_v, cache_mem_k, cache_mem_v, state_conv, state_ffn_conv, norm1_g, w_in, conv_dw_w, conv_dw_b, conv_ln_g, conv_ln_b, w_conv_out, lam_q1, lam_k1, lam_q2, lam_k2, diff_subln_g, w_diff_out, mem_norm_g, w_mem_kv, w_mem_out, w_out, norm2_g, w_ffn_gu, ffn_dw_w, ffn_dw_b, w_ffn_down, final_g):
    depth = w_in.shape[0]
    bp, sp, d = x_prompt.shape
    bs, ss, _ = x_sample.shape
    mlen = mem_prompt.shape[1]
    ktaps, cc = conv_dw_w.shape[1:]
    ff = w_ffn_down.shape[1]
    past = cache_k.shape[2]
    conv_hist = 32

    xp = x_prompt.reshape(bp * sp, d)
    xs = x_sample.reshape(bs * ss, d)
    mem = mem_prompt.reshape(bp * mlen, d)
    fin = _row(final_g)

    outs = {n: [] for n in ("kp", "vp", "mkp", "mvp", "cp", "fp", "ks", "vs", "cs", "fs")}
    for l in range(depth):
        P = {
            "norm1_g": _row(norm1_g[l]),
            "w_in": w_in[l].astype(BF16),
            "conv_dw_w": conv_dw_w[l], "conv_dw_b": _row(conv_dw_b[l]),
            "conv_ln_g": _row(conv_ln_g[l]), "conv_ln_b": _row(conv_ln_b[l]),
            "w_conv_out": w_conv_out[l].astype(BF16),
            "lam": (_row(lam_q1[l]), _row(lam_k1[l]), _row(lam_q2[l]), _row(lam_k2[l])),
            "diff_subln_g": _row(diff_subln_g[l]),
            "w_diff_out": w_diff_out[l].astype(BF16),
            "w_mem_out": w_mem_out[l].astype(BF16),
            "w_out": w_out[l].astype(BF16),
            "norm2_g": _row(norm2_g[l]),
            "w_ffn_gu": w_ffn_gu[l].astype(BF16),
            "ffn_dw_w": ffn_dw_w[l], "ffn_dw_b": _row(ffn_dw_b[l]),
            "w_ffn_down": w_ffn_down[l].astype(BF16),
        }
        final_norm = l == depth - 1

        mkv = _norm_proj(mem, _row(mem_norm_g[l]), w_mem_kv[l].astype(BF16), F32)
        mk_p = mkv[:, :MEM_W].reshape(bp, mlen, MEM_W)
        mv_p = mkv[:, MEM_W:].reshape(bp, mlen, MEM_W)
        xp, kp, vp, cp, fp = _layer(
            xp, bp, sp, l, jnp.zeros((bp, conv_hist, cc), F32), jnp.zeros((bp, STATE_PAD, ff), F32),
            None, None, mk_p.astype(BF16), mv_p.astype(BF16), P, fin, final_norm)
        outs["kp"].append(kp.reshape(bp, sp, DIFF_H, HEAD_W))
        outs["vp"].append(vp.reshape(bp, sp, DIFF_H, HEAD_W))
        outs["mkp"].append(mk_p.reshape(bp, mlen, MEM_H, MEM_DH))
        outs["mvp"].append(mv_p.reshape(bp, mlen, MEM_H, MEM_DH))
        outs["cp"].append(cp[:, conv_hist - (ktaps - 1):])
        outs["fp"].append(fp[:, STATE_PAD - 2:])

        xs, ks_, vs_, cs_, fs_ = _layer(
            xs, bs, ss, l, _pad_state(state_conv[l], conv_hist), _pad_state(state_ffn_conv[l], STATE_PAD),
            cache_k[l].reshape(bs, past, DIFF_W), cache_v[l].reshape(bs, past, DIFF_W),
            cache_mem_k[l].reshape(bs, mlen, MEM_W).astype(BF16),
            cache_mem_v[l].reshape(bs, mlen, MEM_W).astype(BF16), P, fin, final_norm)
        outs["ks"].append(ks_.reshape(bs, ss, DIFF_H, HEAD_W))
        outs["vs"].append(vs_.reshape(bs, ss, DIFF_H, HEAD_W))
        outs["cs"].append(cs_[:, conv_hist - (ktaps - 1):])
        outs["fs"].append(fs_[:, STATE_PAD - 2:])

    st = lambda n: jnp.stack(outs[n])
    return (xp.reshape(bp, sp, d), xs.reshape(bs, ss, d),
            st("kp"), st("vp"), st("mkp"), st("mvp"), st("cp"), st("fp"),
            st("ks"), st("vs"), st("cs"), st("fs"))
```

```python
import functools
import math

import jax
import jax.numpy as jnp
from jax import lax
from jax.experimental import pallas as pl
from jax.experimental.pallas import tpu as pltpu

F32 = jnp.float32
BF16 = jnp.bfloat16

EPS = 1e-6
CHUNK = 64
DIFF_H = 8
DIFF_DH = 64
HEAD_W = 2 * DIFF_DH
DIFF_W = DIFF_H * HEAD_W
MEM_H = 4
MEM_DH = 128
MEM_W = MEM_H * MEM_DH
N_BRANCH = 3
STATE_PAD = 8

VMEM_LIMIT_BYTES = 56 * 1024 * 1024


def _lambda_init(layer_idx):
    return 0.8 - 0.6 * math.exp(-0.3 * layer_idx)


def _params(*sem):
    return pltpu.CompilerParams(dimension_semantics=sem, vmem_limit_bytes=VMEM_LIMIT_BYTES)


def _pick(n, prefs):
    for p in prefs:
        if n % p == 0:
            return p
    return n


def _rms_rows(x, g):
    return x * lax.rsqrt(jnp.mean(x * x, axis=-1, keepdims=True) + EPS) * g


def _kv_proj_kernel(x_ref, g_ref, w_ref, k_ref, v_ref, kb_ref, vb_ref):
    h = _rms_rows(x_ref[...], g_ref[...]).astype(BF16)
    kv = jnp.dot(h, w_ref[...], preferred_element_type=F32)
    k = kv[:, :DIFF_W]
    v = kv[:, DIFF_W:]
    k_ref[...] = k
    v_ref[...] = v
    kb_ref[...] = k.astype(BF16)
    vb_ref[...] = v.astype(BF16)


def _kv_proj(x, g, w_in, kv_blk):
    t, d = x.shape
    tm = _pick(t, (512, 256))
    row = lambda i: (i, 0)
    const = lambda i: (0, 0)
    out_blk = pl.BlockSpec((tm, DIFF_W), row)
    return pl.pallas_call(
        _kv_proj_kernel,
        grid=(t // tm,),
        in_specs=[pl.BlockSpec((tm, d), row), pl.BlockSpec((1, d), const),
                  pl.BlockSpec((d, 2 * DIFF_W), lambda i: (0, kv_blk))],
        out_specs=[out_blk, out_blk, out_blk, out_blk],
        out_shape=[jax.ShapeDtypeStruct((t, DIFF_W), F32), jax.ShapeDtypeStruct((t, DIFF_W), F32),
                   jax.ShapeDtypeStruct((t, DIFF_W), BF16), jax.ShapeDtypeStruct((t, DIFF_W), BF16)],
        compiler_params=_params("parallel"),
        name="kv_proj",
    )(x, g, w_in)


ATT_BLK = 512
VT_ROWS = HEAD_W + 16


def _kv_proj_vt_kernel(x_ref, g_ref, w_ref, *refs, first):
    if first:
        k_ref, v_ref, kb_ref, vt_ref = refs
    else:
        _, _, k_ref, v_ref, kb_ref, vt_ref = refs
    h = _rms_rows(x_ref[...], g_ref[...]).astype(BF16)
    kv = jnp.dot(h, w_ref[...], preferred_element_type=F32)
    k = kv[:, :DIFF_W]
    v = kv[:, DIFF_W:]
    if first:
        for slot in range(k_ref.shape[0]):
            k_ref[slot] = k
            v_ref[slot] = v
    else:
        k_ref[...] = k
        v_ref[...] = v
    kb_ref[...] = k.astype(BF16)
    tm = v.shape[0]
    for hd in range(DIFF_H):
        vt_ref[hd, 0:HEAD_W, :] = v[:, hd * HEAD_W:(hd + 1) * HEAD_W].T.astype(BF16)
        vt_ref[hd, HEAD_W:VT_ROWS, :] = jnp.ones((VT_ROWS - HEAD_W, tm), BF16)


def _kv_proj_vt(x, g, w_in, kv_blk, bsz, seq, layer_idx, depth, kv_all):
    t, d = x.shape
    tm = ATT_BLK
    nb = seq // tm
    row = lambda i: (i, 0)
    const = lambda i: (0, 0)
    first = kv_all is None
    if first:
        assert layer_idx == 0
        kv_spec = pl.BlockSpec((depth, tm, DIFF_W), lambda i: (0, i, 0))
        extra, extra_specs, aliases = (), [], {}
    else:
        kv_spec = pl.BlockSpec((None, tm, DIFF_W), lambda i: (layer_idx, i, 0))
        extra = tuple(kv_all)
        extra_specs = [pl.BlockSpec(memory_space=pl.ANY), pl.BlockSpec(memory_space=pl.ANY)]
        aliases = {3: 0, 4: 1}
    return pl.pallas_call(
        functools.partial(_kv_proj_vt_kernel, first=first),
        grid=(t // tm,),
        in_specs=[pl.BlockSpec((tm, d), row), pl.BlockSpec((1, d), const),
                  pl.BlockSpec((d, 2 * DIFF_W), lambda i: (0, kv_blk))] + extra_specs,
        out_specs=[kv_spec, kv_spec, pl.BlockSpec((tm, DIFF_W), row),
                   pl.BlockSpec((None, DIFF_H, None, VT_ROWS, tm),
                                lambda i: (i // nb, 0, i % nb, 0, 0))],
        out_shape=[jax.ShapeDtypeStruct((depth, t, DIFF_W), F32),
                   jax.ShapeDtypeStruct((depth, t, DIFF_W), F32),
                   jax.ShapeDtypeStruct((t, DIFF_W), BF16),
                   jax.ShapeDtypeStruct((bsz, DIFF_H, nb, VT_ROWS, tm), BF16)],
        input_output_aliases=aliases,
        compiler_params=_params("parallel"),
        name="kv_proj_vt",
    )(x, g, w_in, *extra)


def _norm_proj_kernel(x_ref, g_ref, w_ref, o_ref, h_ref):
    @pl.when(pl.program_id(1) == 0)
    def _():
        h_ref[...] = _rms_rows(x_ref[...], g_ref[...]).astype(BF16)

    o_ref[...] = jnp.dot(h_ref[...], w_ref[...], preferred_element_type=F32).astype(o_ref.dtype)


PROJ_TN = 512


def _norm_proj(x, g, w, out_dtype, col_tiles=None):
    t, d = x.shape
    tm = _pick(t, (1024, 512, 256))
    if col_tiles is None:
        n = w.shape[1]
        tn = _pick(n, (PROJ_TN, 256, 128))
        col_tiles = (n // tn, lambda j: j)
    else:
        tn = PROJ_TN
    nn, col_map = col_tiles
    n = nn * tn
    return pl.pallas_call(
        _norm_proj_kernel,
        grid=(t // tm, nn),
        in_specs=[pl.BlockSpec((tm, d), lambda i, j: (i, 0)),
                  pl.BlockSpec((1, d), lambda i, j: (0, 0)),
                  pl.BlockSpec((d, tn), lambda i, j: (0, col_map(j)))],
        out_specs=pl.BlockSpec((tm, tn), lambda i, j: (i, j)),
        out_shape=jax.ShapeDtypeStruct((t, n), out_dtype),
        scratch_shapes=[pltpu.VMEM((tm, d), BF16)],
        compiler_params=_params("parallel", "arbitrary"),
        name="norm_proj",
    )(x, g, w)


CONV_ROWS = 32


def _conv_branch_kernel(cin_ref, ctx_ref, w_ref, b_ref, lg_ref, lb_ref, c_ref, st_ref, ubuf, sh_ref,
                        *, tm, cc, ktaps, hist):
    s = pl.program_id(1)
    lead = hist - (ktaps - 1)
    sub = STATE_PAD

    @pl.when(s == 0)
    def _():
        ubuf[0:hist, :] = ctx_ref[...]

    cin = cin_ref[...].astype(F32)
    u = cin[:, :cc] * jax.nn.sigmoid(cin[:, cc:])
    ubuf[hist:hist + tm, :] = u
    span = hist + tm - sub
    for r in range(1, sub):
        sh_ref[r - 1, 0:span, :] = ubuf[r:r + span, :]

    bias = b_ref[...]
    lg = lg_ref[...]
    lb = lb_ref[...]
    for r0 in range(0, tm, CONV_ROWS):
        acc = jnp.zeros((CONV_ROWS, cc), F32) + bias
        for k in range(ktaps):
            r = (lead + k) % sub
            a = lead + k - r + r0
            if r == 0:
                tap = ubuf[a:a + CONV_ROWS, :]
            else:
                tap = sh_ref[r - 1, a:a + CONV_ROWS, :]
            acc = acc + w_ref[k:k + 1, :] * tap
        mu = jnp.mean(acc, axis=-1, keepdims=True)
        xc = acc - mu
        var = jnp.mean(xc * xc, axis=-1, keepdims=True)
        y = xc * lax.rsqrt(var + EPS) * lg + lb
        c_ref[r0:r0 + CONV_ROWS, :] = (y * jax.nn.sigmoid(y)).astype(c_ref.dtype)

    tail = ubuf[tm:tm + hist, :]
    st_ref[...] = tail
    ubuf[0:hist, :] = tail


def _conv_branch(slab, col_blk, ctx, dw_w, dw_b, ln_g, ln_b, bsz, seq):
    ktaps, cc = dw_w.shape
    hist = 32
    assert ktaps - 1 <= hist and ctx.shape == (bsz, hist, cc)
    tm = _pick(seq, (512, 256, 128, 64, 32))
    nt = seq // tm
    kern = functools.partial(_conv_branch_kernel, tm=tm, cc=cc, ktaps=ktaps, hist=hist)
    vec = pl.BlockSpec((1, cc), lambda b, s: (0, 0))
    return pl.pallas_call(
        kern,
        grid=(bsz, nt),
        in_specs=[pl.BlockSpec((tm, 2 * cc), lambda b, s: (b * nt + s, col_blk)),
                  pl.BlockSpec((None, hist, cc), lambda b, s: (b, 0, 0)),
                  pl.BlockSpec((ktaps, cc), lambda b, s: (0, 0)),
                  vec, vec, vec],
        out_specs=[pl.BlockSpec((tm, cc), lambda b, s: (b * nt + s, 0)),
                   pl.BlockSpec((None, hist, cc), lambda b, s: (b, 0, 0))],
        out_shape=[jax.ShapeDtypeStruct((bsz * seq, cc), BF16),
                   jax.ShapeDtypeStruct((bsz, hist, cc), F32)],
        scratch_shapes=[pltpu.VMEM((hist + tm, cc), F32),
                        pltpu.VMEM((STATE_PAD - 1, hist + tm - STATE_PAD, cc), F32)],
        compiler_params=_params("arbitrary", "arbitrary"),
        name="conv_branch",
    )(slab, ctx, dw_w, dw_b, ln_g, ln_b)


def _full_attn_kernel(q_ref, kp_ref, vp_ref, kn_ref, vn_ref, lq1_ref, lk1_ref, lq2_ref, lk2_ref,
                      g_ref, o_ref, qq_ref, *, bq, l_init):
    lane = lax.broadcasted_iota(jnp.int32, (bq, HEAD_W), 1)
    q = q_ref[...] * (DIFF_DH ** -0.5)
    zero = jnp.zeros_like(q)
    qq_ref[0:bq, :] = jnp.where(lane < DIFF_DH, q, zero)
    qq_ref[bq:2 * bq, :] = jnp.where(lane >= DIFF_DH, q, zero)
    nt_dims = (((1,), (1,)), ((), ()))
    qq = qq_ref[...]
    s_p = lax.dot_general(qq, kp_ref[...].astype(BF16), nt_dims, preferred_element_type=F32)
    s_n = lax.dot_general(qq, kn_ref[...], nt_dims, preferred_element_type=F32)
    m = jnp.maximum(jnp.max(s_p, axis=-1, keepdims=True), jnp.max(s_n, axis=-1, keepdims=True))
    p_p = jnp.exp(s_p - m)
    p_n = jnp.exp(s_n - m)
    l = jnp.sum(p_p, axis=-1, keepdims=True) + jnp.sum(p_n, axis=-1, keepdims=True)
    o_all = (jnp.dot(p_p.astype(BF16), vp_ref[...].astype(BF16), preferred_element_type=F32)
             + jnp.dot(p_n.astype(BF16), vn_ref[...], preferred_element_type=F32)) / l
    lam = (jnp.exp(jnp.sum(lq1_ref[...] * lk1_ref[...], axis=-1, keepdims=True))
           - jnp.exp(jnp.sum(lq2_ref[...] * lk2_ref[...], axis=-1, keepdims=True)) + l_init)
    o = o_all[0:bq, :] - lam * o_all[bq:2 * bq, :]
    o_ref[...] = (_rms_rows(o, g_ref[...]) * (1.0 - l_init)).astype(o_ref.dtype)


def _causal_attn_kernel(q_ref, k_ref, vt_ref, lq1_ref, lk1_ref, lq2_ref, lk2_ref, g_ref, o_ref,
                        qq_ref, sa_ref, sb_ref, ma_ref, mb_ref, m_ref, acc_ref, *, blk, l_init):
    i = pl.program_id(2)
    lane = lax.broadcasted_iota(jnp.int32, (blk, HEAD_W), 1)
    q = q_ref[...] * (DIFF_DH ** -0.5)
    zero = jnp.zeros_like(q)
    qq_ref[0:blk, :] = jnp.where(lane < DIFF_DH, q, zero)
    qq_ref[blk:2 * blk, :] = jnp.where(lane >= DIFF_DH, q, zero)
    m_ref[...] = jnp.full(m_ref.shape, -jnp.inf, F32)
    acc_ref[...] = jnp.zeros(acc_ref.shape, F32)

    def scores(j, s_ref, mx_ref):
        start = pl.multiple_of(j * blk, blk)
        st = lax.dot_general(k_ref[pl.ds(start, blk), :], qq_ref[...], (((1,), (1,)), ((), ())),
                             preferred_element_type=F32)
        s_ref[...] = st
        mx_ref[...] = jnp.max(st, axis=0, keepdims=True)

    def update(j, s_ref, mx_ref, masked):
        st = s_ref[...]
        if masked:
            kchunk = lax.broadcasted_iota(jnp.int32, (blk, 1), 0) // CHUNK
            qcol = lax.broadcasted_iota(jnp.int32, (1, 2 * blk), 1)
            qchunk = jnp.where(qcol >= blk, qcol - blk, qcol) // CHUNK
            st = jnp.where(kchunk <= qchunk, st, -jnp.inf)
            mx = jnp.max(st, axis=0, keepdims=True)
        else:
            mx = mx_ref[...]
        m_prev = m_ref[...]
        m_new = jnp.maximum(m_prev, mx)
        alpha = jnp.exp(m_prev - m_new)
        pt = jnp.exp(st - m_new).astype(BF16)
        acc_ref[...] = alpha * acc_ref[...] + jnp.dot(vt_ref[j], pt, preferred_element_type=F32)
        m_ref[...] = m_new

    scores(0, sa_ref, ma_ref)

    def pair(jp, carry):
        j = 2 * jp
        scores(j + 1, sb_ref, mb_ref)
        update(j, sa_ref, ma_ref, False)
        scores(j + 2, sa_ref, ma_ref)
        update(j + 1, sb_ref, mb_ref, False)
        return carry

    lax.fori_loop(0, i // 2, pair, 0)

    @pl.when(i % 2 == 1)
    def _():
        scores(i, sb_ref, mb_ref)
        update(i - 1, sa_ref, ma_ref, False)
        update(i, sb_ref, mb_ref, True)

    @pl.when(i % 2 == 0)
    def _():
        update(i, sa_ref, ma_ref, True)

    lam = (jnp.exp(jnp.sum(lq1_ref[...] * lk1_ref[...], axis=-1, keepdims=True))
           - jnp.exp(jnp.sum(lq2_ref[...] * lk2_ref[...], axis=-1, keepdims=True)) + l_init)
    acc = acc_ref[...]
    o_all = acc[0:HEAD_W, :] / acc[HEAD_W:HEAD_W + 1, :]
    ot = o_all[:, 0:blk] - lam * o_all[:, blk:2 * blk]
    ot = ot * lax.rsqrt(jnp.mean(ot * ot, axis=0, keepdims=True) + EPS) * (1.0 - l_init)
    o_ref[...] = (ot.T * g_ref[...]).astype(o_ref.dtype)


def _causal_attn(slab, q_col0, kb, vt, lam_p, subln_g, bsz, seq, l_init):
    blk = ATT_BLK
    assert seq % blk == 0 and blk % CHUNK == 0
    nq = seq // blk
    kern = functools.partial(_causal_attn_kernel, blk=blk, l_init=l_init)
    lam_blk = pl.BlockSpec((1, DIFF_DH), lambda b, h, i: (0, 0))
    return pl.pallas_call(
        kern,
        grid=(bsz, DIFF_H, nq),
        in_specs=[pl.BlockSpec((blk, HEAD_W), lambda b, h, i: (b * nq + i, q_col0 + h)),
                  pl.BlockSpec((None, seq, HEAD_W), lambda b, h, i: (b, 0, h)),
                  pl.BlockSpec((None, None, nq, VT_ROWS, blk), lambda b, h, i: (b, h, 0, 0, 0)),
                  lam_blk, lam_blk, lam_blk, lam_blk,
                  pl.BlockSpec((1, HEAD_W), lambda b, h, i: (0, 0))],
        out_specs=pl.BlockSpec((blk, HEAD_W), lambda b, h, i: (b * nq + i, h)),
        out_shape=jax.ShapeDtypeStruct((bsz * seq, DIFF_W), BF16),
        scratch_shapes=[pltpu.VMEM((2 * blk, HEAD_W), BF16),
                        pltpu.VMEM((blk, 2 * blk), F32), pltpu.VMEM((blk, 2 * blk), F32),
                        pltpu.VMEM((1, 2 * blk), F32), pltpu.VMEM((1, 2 * blk), F32),
                        pltpu.VMEM((1, 2 * blk), F32),
                        pltpu.VMEM((VT_ROWS, 2 * blk), F32)],
        compiler_params=_params("parallel", "parallel", "arbitrary"),
        name="causal_attn",
    )(slab, kb, vt, *lam_p, subln_g)


def _full_attn(slab, q_col0, k_past, v_past, kb, vb, lam_p, subln_g, bsz, sq, l_init):
    past = k_past.shape[1]
    kern = functools.partial(_full_attn_kernel, bq=sq, l_init=l_init)
    lam_blk = pl.BlockSpec((1, DIFF_DH), lambda b, h: (0, 0))
    past_blk = pl.BlockSpec((None, past, HEAD_W), lambda b, h: (b, 0, h))
    new_blk = pl.BlockSpec((sq, HEAD_W), lambda b, h: (b, h))
    return pl.pallas_call(
        kern,
        grid=(bsz, DIFF_H),
        in_specs=[pl.BlockSpec((sq, HEAD_W), lambda b, h: (b, q_col0 + h)),
                  past_blk, past_blk, new_blk, new_blk, lam_blk, lam_blk, lam_blk, lam_blk,
                  pl.BlockSpec((1, HEAD_W), lambda b, h: (0, 0))],
        out_specs=pl.BlockSpec((sq, HEAD_W), lambda b, h: (b, h)),
        out_shape=jax.ShapeDtypeStruct((bsz * sq, DIFF_W), BF16),
        scratch_shapes=[pltpu.VMEM((2 * sq, HEAD_W), BF16)],
        compiler_params=_params("parallel", "parallel"),
        name="full_attn",
    )(slab, k_past, v_past, kb, vb, *lam_p, subln_g)


def _merge_kernel(x_ref, c_ref, on_ref, mq_ref, gate_ref, mk_ref, mv_ref,
                  wco_ref, wdo_ref, wmo_ref, wout_ref, o_ref, om_ref, *, d):
    scale = MEM_DH ** -0.5
    for h in range(MEM_H):
        cols = slice(h * MEM_DH, (h + 1) * MEM_DH)
        s = lax.dot_general(mq_ref[:, cols], mk_ref[:, cols], (((1,), (1,)), ((), ())),
                            preferred_element_type=F32) * scale
        p = jnp.exp(s - jnp.max(s, axis=-1, keepdims=True))
        p = p / jnp.sum(p, axis=-1, keepdims=True)
        om_ref[:, cols] = jnp.dot(p.astype(BF16), mv_ref[:, cols],
                                  preferred_element_type=F32).astype(BF16)

    def gate(b):
        return jax.nn.sigmoid(gate_ref[:, b * d:(b + 1) * d].astype(F32))

    mix = gate(0) * jnp.dot(c_ref[...], wco_ref[...], preferred_element_type=F32)
    mix = mix + gate(1) * jnp.dot(on_ref[...], wdo_ref[...], preferred_element_type=F32)
    mix = mix + gate(2) * jnp.dot(om_ref[...], wmo_ref[...], preferred_element_type=F32)
    o_ref[...] = x_ref[...] + jnp.dot(mix.astype(BF16), wout_ref[...], preferred_element_type=F32)


def _merge(x, c, on, slab, mq_blk, mk, mv, wco, wdo, wmo, wout, bsz, seq):
    t, d = x.shape
    cc = c.shape[1]
    mlen = mk.shape[1]
    tm = _pick(seq, (256, 128, 64, 32))
    nt = seq // tm
    row = lambda i: (i, 0)
    full = lambda a: pl.BlockSpec(a.shape, lambda i: (0, 0))
    mem_blk = pl.BlockSpec((None, mlen, MEM_W), lambda i: (i // nt, 0, 0))
    return pl.pallas_call(
        functools.partial(_merge_kernel, d=d),
        grid=(t // tm,),
        in_specs=[pl.BlockSpec((tm, d), row), pl.BlockSpec((tm, cc), row),
                  pl.BlockSpec((tm, DIFF_W), row),
                  pl.BlockSpec((tm, MEM_W), lambda i: (i, mq_blk)),
                  pl.BlockSpec((tm, N_BRANCH * d), row),
                  mem_blk, mem_blk, full(wco), full(wdo), full(wmo), full(wout)],
        out_specs=pl.BlockSpec((tm, d), row),
        out_shape=jax.ShapeDtypeStruct((t, d), F32),
        scratch_shapes=[pltpu.VMEM((tm, MEM_W), BF16)],
        compiler_params=_params("parallel"),
        name="merge",
    )(x, c, on, slab, slab, mk, mv, wco, wdo, wmo, wout)


def _ffn_body(x_ref, g_ref, wg_ref, wu_ref, wd_ref, dw_ref, db_ref, ctx_ref, fin_ref,
              o_ref, st_ref, h_ref, ext_ref, tail_ref, acta_ref, actb_ref,
              *, tm, tf, nt, nf, final_norm):
    i = pl.program_id(0)
    j = pl.program_id(1)
    s = i % nt

    def gate_up(act_ref):
        col = pl.ds(pl.multiple_of(j * tf, tf), tf)
        h = h_ref[...]
        fg = jnp.dot(h, wg_ref[...], preferred_element_type=F32)
        fu = jnp.dot(h, wu_ref[...], preferred_element_type=F32)
        ext_ref[0:STATE_PAD, :] = jnp.where(s == 0, ctx_ref[:, col], tail_ref[j])
        ext_ref[STATE_PAD:STATE_PAD + tm, :] = fg
        last = fg[tm - STATE_PAD:tm, :]
        tail_ref[j] = last
        st_ref[:, col] = last
        y = (dw_ref[0:1, col] * ext_ref[STATE_PAD - 2:STATE_PAD - 2 + tm, :]
             + dw_ref[1:2, col] * ext_ref[STATE_PAD - 1:STATE_PAD - 1 + tm, :]
             + dw_ref[2:3, col] * fg + db_ref[:, col])
        act_ref[...] = (y * jax.nn.sigmoid(y) * fu).astype(BF16)

    def down(act_ref):
        o_ref[...] += jnp.dot(act_ref[...], wd_ref[...], preferred_element_type=F32)

    slots = (acta_ref, actb_ref)

    @pl.when(j == 0)
    def _():
        @pl.when(i == 0)
        def _():
            tail_ref[...] = jnp.zeros(tail_ref.shape, F32)

        h_ref[...] = _rms_rows(x_ref[...], g_ref[...]).astype(BF16)
        o_ref[...] = jnp.zeros(o_ref.shape, F32)
        gate_up(slots[0])

    for parity in (0, 1):
        @pl.when((j > 0) & (j < nf) & (j % 2 == parity))
        def _():
            down(slots[1 - parity])
            gate_up(slots[parity])

    @pl.when(j == nf)
    def _():
        down(slots[(nf - 1) % 2])
        out = x_ref[...] + o_ref[...]
        if final_norm:
            out = _rms_rows(out, fin_ref[...])
        o_ref[...] = out


def _ffn(x, g, wgu, wd, dw_w, dw_b, ctx, fin_g, bsz, seq, final_norm):
    t, d = x.shape
    ff = wd.shape[0]
    assert dw_w.shape[0] == 3 and ctx.shape == (bsz, STATE_PAD, ff)
    tm = _pick(seq, (1024, 512, 256, 128, 64, 32))
    tf = _pick(ff, (512, 256, 128))
    nt, nf = seq // tm, ff // tf
    assert nf >= 2
    kern = functools.partial(_ffn_body, tm=tm, tf=tf, nt=nt, nf=nf, final_norm=final_norm)
    cur = lambda j: jnp.minimum(j, nf - 1)
    prev = lambda j: jnp.maximum(j - 1, 0)
    whole = lambda a: pl.BlockSpec(a.shape, lambda i, j: (0, 0))
    return pl.pallas_call(
        kern,
        grid=(t // tm, nf + 1),
        in_specs=[pl.BlockSpec((tm, d), lambda i, j: (i, 0), pipeline_mode=pl.Buffered(1)),
                  whole(g),
                  pl.BlockSpec((d, tf), lambda i, j: (0, cur(j))),
                  pl.BlockSpec((d, tf), lambda i, j: (0, nf + cur(j))),
                  pl.BlockSpec((tf, d), lambda i, j: (prev(j), 0)),
                  whole(dw_w), whole(dw_b),
                  pl.BlockSpec((None, STATE_PAD, ff), lambda i, j: (i // nt, 0, 0)),
                  whole(fin_g)],
        out_specs=[pl.BlockSpec((tm, d), lambda i, j: (i, 0)),
                   pl.BlockSpec((None, STATE_PAD, ff), lambda i, j: (i // nt, 0, 0))],
        out_shape=[jax.ShapeDtypeStruct((t, d), F32),
                   jax.ShapeDtypeStruct((bsz, STATE_PAD, ff), F32)],
        scratch_shapes=[pltpu.VMEM((tm, d), BF16),
                        pltpu.VMEM((STATE_PAD + tm, tf), F32),
                        pltpu.VMEM((nf, STATE_PAD, tf), F32),
                        pltpu.VMEM((tm, tf), BF16), pltpu.VMEM((tm, tf), BF16)],
        compiler_params=_params("arbitrary", "arbitrary"),
        name="conv_ffn",
    )(x, g, wgu, wgu, wd, dw_w, dw_b, ctx, fin_g)


def _row(v):
    return v.reshape(1, -1)


def _pad_state(st, rows):
    return jnp.pad(st, ((0, 0), (rows - st.shape[1], 0), (0, 0)))


def _layer(x, bsz, seq, layer_idx, conv_ctx, ffn_ctx, k_past, v_past, mk, mv, P, final_g, final_norm,
           depth=None, kv_all=None):
    t, d = x.shape
    cc = P["conv_dw_w"].shape[1]
    l_init = _lambda_init(layer_idx)

    gate_w = N_BRANCH * d
    front = 2 * cc + DIFF_W
    for edge in (front, 2 * DIFF_W, MEM_W, gate_w):
        assert edge % PROJ_TN == 0
    g_t, front_t = gate_w // PROJ_TN, front // PROJ_TN
    mq0 = (front + 2 * DIFF_W) // PROJ_TN
    gates0 = mq0 + MEM_W // PROJ_TN
    col_map = lambda j: jnp.where(j < g_t, j + gates0,
                                  jnp.where(j < g_t + front_t, j - g_t, j - g_t - front_t + mq0))
    slab = _norm_proj(x, P["norm1_g"], P["w_in"], BF16,
                      col_tiles=(g_t + front_t + MEM_W // PROJ_TN, col_map))
    conv_blk = gate_w // (2 * cc)
    q_col0 = (gate_w + 2 * cc) // HEAD_W
    mq_blk = (gate_w + 2 * cc + DIFF_W) // MEM_W
    assert gate_w % (2 * cc) == 0 and (gate_w + 2 * cc + DIFF_W) % MEM_W == 0
    assert front % (2 * DIFF_W) == 0
    kv_blk = front // (2 * DIFF_W)

    c, conv_state = _conv_branch(slab, conv_blk, conv_ctx, P["conv_dw_w"], P["conv_dw_b"],
                                 P["conv_ln_g"], P["conv_ln_b"], bsz, seq)

    if k_past is None:
        k, v, kb, vt = _kv_proj_vt(x, P["norm1_g"], P["w_in"], kv_blk, bsz, seq,
                                   layer_idx, depth, kv_all)
        on = _causal_attn(slab, q_col0, kb.reshape(bsz, seq, DIFF_W), vt, P["lam"],
                          P["diff_subln_g"], bsz, seq, l_init)
    else:
        k, v, kb, vb = _kv_proj(x, P["norm1_g"], P["w_in"], kv_blk)
        on = _full_attn(slab, q_col0, k_past, v_past, kb, vb, P["lam"],
                        P["diff_subln_g"], bsz, seq, l_init)

    x1 = _merge(x, c, on, slab, mq_blk, mk, mv, P["w_conv_out"], P["w_diff_out"],
                P["w_mem_out"], P["w_out"], bsz, seq)
    x2, ffn_state = _ffn(x1, P["norm2_g"], P["w_ffn_gu"], P["w_ffn_down"], P["ffn_dw_w"],
                         P["ffn_dw_b"], ffn_ctx, final_g, bsz, seq, final_norm)
    return x2, k, v, conv_state, ffn_state


def kernel(x_prompt, x_sample, mem_prompt, cache_k, cache_v, cache_mem_k, cache_mem_v, state_conv, state_ffn_conv, norm1_g, w_in, conv_dw_w, conv_dw_b, conv_ln_g, conv_ln_b, w_conv_out, lam_q1, lam_k1, lam_q2, lam_k2, diff_subln_g, w_diff_out, mem_norm_g, w_mem_kv, w_mem_out, w_out, norm2_g, w_ffn_gu, ffn_dw_w, ffn_dw_b, w_ffn_down, final_g):
    depth = w_in.shape[0]
    bp, sp, d = x_prompt.shape
    bs, ss, _ = x_sample.shape
    mlen = mem_prompt.shape[1]
    ktaps, cc = conv_dw_w.shape[1:]
    ff = w_ffn_down.shape[1]
    past = cache_k.shape[2]
    conv_hist = 32

    xp = x_prompt.reshape(bp * sp, d)
    xs = x_sample.reshape(bs * ss, d)
    mem = mem_prompt.reshape(bp * mlen, d)
    fin = _row(final_g)

    outs = {n: [] for n in ("mkp", "mvp", "cp", "fp", "ks", "vs", "cs", "fs")}
    for l in range(depth):
        P = {
            "norm1_g": _row(norm1_g[l]),
            "w_in": w_in[l].astype(BF16),
            "conv_dw_w": conv_dw_w[l], "conv_dw_b": _row(conv_dw_b[l]),
            "conv_ln_g": _row(conv_ln_g[l]), "conv_ln_b": _row(conv_ln_b[l]),
            "w_conv_out": w_conv_out[l].astype(BF16),
            "lam": (_row(lam_q1[l]), _row(lam_k1[l]), _row(lam_q2[l]), _row(lam_k2[l])),
            "diff_subln_g": _row(diff_subln_g[l]),
            "w_diff_out": w_diff_out[l].astype(BF16),
            "w_mem_out": w_mem_out[l].astype(BF16),
            "w_out": w_out[l].astype(BF16),
            "norm2_g": _row(norm2_g[l]),
            "w_ffn_gu": w_ffn_gu[l].astype(BF16),
            "ffn_dw_w": ffn_dw_w[l], "ffn_dw_b": _row(ffn_dw_b[l]),
            "w_ffn_down": w_ffn_down[l].astype(BF16),
        }
        final_norm = l == depth - 1

        mkv = _norm_proj(mem, _row(mem_norm_g[l]), w_mem_kv[l].astype(BF16), F32)
        mk_p = mkv[:, :MEM_W].reshape(bp, mlen, MEM_W)
        mv_p = mkv[:, MEM_W:].reshape(bp, mlen, MEM_W)
        xp, kp, vp, cp, fp = _layer(
            xp, bp, sp, l, jnp.zeros((bp, conv_hist, cc), F32), jnp.zeros((bp, STATE_PAD, ff), F32),
            None, None, mk_p.astype(BF16), mv_p.astype(BF16), P, fin, final_norm,
            depth=depth, kv_all=None if l == 0 else (kp, vp))
        outs["mkp"].append(mk_p.reshape(bp, mlen, MEM_H, MEM_DH))
        outs["mvp"].append(mv_p.reshape(bp, mlen, MEM_H, MEM_DH))
        outs["cp"].append(cp[:, conv_hist - (ktaps - 1):])
        outs["fp"].append(fp[:, STATE_PAD - 2:])

        xs, ks_, vs_, cs_, fs_ = _layer(
            xs, bs, ss, l, _pad_state(state_conv[l], conv_hist), _pad_state(state_ffn_conv[l], STATE_PAD),
            cache_k[l].reshape(bs, past, DIFF_W), cache_v[l].reshape(bs, past, DIFF_W),
            cache_mem_k[l].reshape(bs, mlen, MEM_W).astype(BF16),
            cache_mem_v[l].reshape(bs, mlen, MEM_W).astype(BF16), P, fin, final_norm)
        outs["ks"].append(ks_.reshape(bs, ss, DIFF_H, HEAD_W))
        outs["vs"].append(vs_.reshape(bs, ss, DIFF_H, HEAD_W))
        outs["cs"].append(cs_[:, conv_hist - (ktaps - 1):])
        outs["fs"].append(fs_[:, STATE_PAD - 2:])

    st = lambda n: jnp.stack(outs[n])
    return (xp.reshape(bp, sp, d), xs.reshape(bs, ss, d),
            kp.reshape(depth, bp, sp, DIFF_H, HEAD_W), vp.reshape(depth, bp, sp, DIFF_H, HEAD_W),
            st("mkp"), st("mvp"), st("cp"), st("fp"),
            st("ks"), st("vs"), st("cs"), st("fs"))
```
